```python
import jax, jax.numpy as jnp
from jax import lax
import numpy as np

D_MODEL = 1024
BATCH = 32
SEQ = 256
DEPTH = 2
DEC_BATCH = 2
DEC_SEQ = 1024
PAST_LEN = 512

GRID_W = 64
N_A_LAYERS = (DEPTH + 1) // 2
N_C_LAYERS = DEPTH // 2
MIX_W = D_MODEL
A_W = MIX_W // 2
HD_A = 64
H_A = A_W // HD_A
W_LORA = 64
A_LORA = 64
G_LORA = 128
A_COLS = 3 * A_W + 2 * W_LORA + 2 * A_LORA + G_LORA
B_W = MIX_W - A_W
H_B = 4
GD_B = B_W // H_B
CHUNK = 128
AB_COLS = A_COLS + 2 * B_W
CONV_W = 31
CONV_HALF = CONV_W // 2
N_KEYS = 128
N_EXPERTS = N_KEYS * N_KEYS
PEER_HEADS = 8
PEER_TOPK = 16
D_QUERY = 256
D_HALF = D_QUERY // 2
PEER_BLOCK = 128
EPS = 1e-6
LN_EPS = 1e-5
GN_EPS = 64e-5

kernel_name = 'hybrid_rwkv7_gmlp_conformer_peer_dit_step'


def _rmsnorm(x, g):
    xf = x.astype(jnp.float32)
    y = xf * lax.rsqrt(jnp.mean(xf * xf, axis=-1, keepdims=True) + EPS)
    return (y * g.astype(jnp.float32)).astype(x.dtype)


def _layernorm(x, g, b, eps=LN_EPS):
    xf = x.astype(jnp.float32)
    m = jnp.mean(xf, axis=-1, keepdims=True)
    v = jnp.mean(jnp.square(xf - m), axis=-1, keepdims=True)
    return ((xf - m) * lax.rsqrt(v + eps) * g.astype(jnp.float32) + b.astype(jnp.float32)).astype(x.dtype)


def _modulation(cvec, w, b):
    return (jax.nn.silu(cvec) @ w + b)[:, None, :]


def _centred_shift(z):
    zp = jnp.pad(z, ((0, 0), (1, 1), (0, 0)))
    return 0.5 * (zp[:, :-2] + zp[:, 2:])


def _wkv_scan(r, w, k, v, a, b, s0, reverse):
    def step(S, inp):
        rt, wt, kt, vt, at, bt = inp
        sa = jnp.einsum('nhij,nhj->nhi', S, at)
        S = S * wt[:, :, None, :] + sa[..., None] * bt[:, :, None, :] + vt[..., None] * kt[:, :, None, :]
        return S, jnp.einsum('nhij,nhj->nhi', S, rt)
    xs = tuple(jnp.moveaxis(t, 1, 0) for t in (r, w, k, v, a, b))
    s_fin, y = lax.scan(step, s0.astype(jnp.float32), xs, reverse=reverse)
    return s_fin, jnp.moveaxis(y, 0, 1)


def _rwkv7_bidir(za, s0, i, p):
    n, T, _ = za.shape
    za = za.astype(jnp.float32)
    za = za + p['mu_a'][i] * (_centred_shift(za) - za)
    sizes = [A_W, A_W, A_W, W_LORA, W_LORA, A_LORA, A_LORA]
    splits = [int(s) for s in np.cumsum(sizes)]
    r, k, v, wlo_f, wlo_b, alo_f, alo_b, glo = jnp.split(za, splits, axis=-1)
    hs = lambda t: t.reshape(n, T, H_A, HD_A)
    g = jax.nn.sigmoid(glo) @ p['g2'][i]
    kk = hs(k * p['k_k'][i])
    kk = kk / jnp.maximum(jnp.sqrt(jnp.sum(kk * kk, axis=-1, keepdims=True)), 1e-12)
    y_sum = 0.0
    bonus = 0.0
    finals = []
    for d, (wlo, alo, rev) in enumerate(((wlo_f, alo_f, False), (wlo_b, alo_b, True))):
        w_log = -jax.nn.softplus(-(p['w0'][i, d] + jnp.tanh(wlo) @ p['w2'][i, d])) - 0.5
        decay = jnp.exp(-jnp.exp(w_log))
        a = jax.nn.sigmoid(p['a0'][i, d] + alo @ p['a2'][i, d])
        kd = k * (1.0 + (a - 1.0) * p['k_a'][i])
        s_fin, y = _wkv_scan(hs(r), hs(decay), hs(kd), hs(v), -kk, kk * hs(a), s0[:, d], rev)
        y_sum = y_sum + y
        bonus = bonus + jnp.sum(hs(r * kd * p['r_k'][i]), axis=-1, keepdims=True) * hs(v)
        finals.append(s_fin)
    m = jnp.mean(y_sum, axis=-1, keepdims=True)
    var = jnp.mean(jnp.square(y_sum - m), axis=-1, keepdims=True)
    gn = ((y_sum - m) * lax.rsqrt(var + GN_EPS)).reshape(n, T, A_W) * p['lnx_g'][i] + p['lnx_b'][i]
    out = (gn + bonus.reshape(n, T, A_W)) * g
    return out, jnp.stack(finals, axis=1)


def _chunk_gmlp(u, vb, i, p):
    n, T, _ = u.shape
    vb = _layernorm(vb.reshape(n, T, H_B, GD_B), p['lnv_g'][i].reshape(H_B, GD_B), p['lnv_b'][i].reshape(H_B, GD_B))
    vb = vb.reshape(n, T // CHUNK, CHUNK, H_B, GD_B)
    sp = jnp.einsum('gpq,ncqgd->ncpgd', p['w_s'][i], vb) + p['b_s'][i].T[None, None, :, :, None]
    return u * sp.reshape(n, T, B_W)


def _mixer_ab(h, i, p, s0):
    z = h @ p['w_in_ab'][i]
    za = z[..., :A_COLS]
    u = z[..., A_COLS:A_COLS + B_W]
    vb = z[..., A_COLS + B_W:]
    ya, s_fin = _rwkv7_bidir(za, s0, i, p)
    yb = _chunk_gmlp(u, vb, i, p)
    y = jnp.concatenate([ya.astype(h.dtype), yb.astype(h.dtype)], axis=-1) @ p['w_out_ab'][i]
    return y, s_fin


def _conformer_conv(h, i, p, seg_len):
    n, T, D = h.shape
    z = h @ p['w_pw1'][i] + p['b_pw1'][i]
    z = z[..., :D] * jax.nn.sigmoid(z[..., D:])
    segs = T // seg_len
    zr = z.reshape(n * segs, seg_len, D)
    zr = lax.conv_general_dilated(zr, p['conv_k'][i][:, None, :].astype(zr.dtype), window_strides=(1,),
                                  padding=((CONV_HALF, CONV_HALF),), dimension_numbers=('NWC', 'WIO', 'NWC'),
                                  feature_group_count=D) + p['conv_b'][i]
    z = _layernorm(zr.reshape(n, T, D), p['lnc_g'][i], p['lnc_b'][i])
    return jax.nn.silu(z) @ p['w_pw2'][i] + p['b_pw2'][i]


def _peer(h, l, p):
    n, T, D = h.shape
    wq = p['w_query'][l]
    sk = p['sub_keys'][l].astype(jnp.float32)
    eu = p['expert_u'][l]
    ev = p['expert_v'][l]

    def block(xb):
        P = xb.shape[0]
        q = (xb @ wq).reshape(P, PEER_HEADS, 2, D_HALF).astype(jnp.float32)
        s = jnp.einsum('phcd,ckd->phck', q, sk)
        sv, si = lax.top_k(s, PEER_TOPK)
        cand = (sv[:, :, 0, :, None] + sv[:, :, 1, None, :]).reshape(P, PEER_HEADS, PEER_TOPK * PEER_TOPK)
        cidx = (si[:, :, 0, :, None] * N_KEYS + si[:, :, 1, None, :]).reshape(P, PEER_HEADS, PEER_TOPK * PEER_TOPK)
        fv, fi = lax.top_k(cand, PEER_TOPK)
        eidx = jnp.take_along_axis(cidx, fi, axis=-1)
        gate = jax.nn.softmax(fv, axis=-1)
        ue = eu[eidx]
        ve = ev[eidx]
        act = jax.nn.gelu(jnp.einsum('phkd,pd->phk', ue, xb).astype(jnp.float32), approximate=False)
        return jnp.einsum('phk,phkd->pd', (gate * act).astype(xb.dtype), ve)

    out = lax.map(block, h.reshape(-1, PEER_BLOCK, D))
    return out.reshape(n, T, D)


def _layer(x, mod, l, p, s0, seg_len):
    sh1, sc1, gt1, sh2, sc2, gt2 = jnp.split(mod.astype(x.dtype), 6, axis=-1)
    h = _rmsnorm(x, p['norm1'][l]) * (1 + sc1) + sh1
    s_fin = None
    if l % 2 == 0:
        y, s_fin = _mixer_ab(h, l // 2, p, s0)
    else:
        y = _conformer_conv(h, l // 2, p, seg_len)
    x = x + gt1 * y
    h = _rmsnorm(x, p['norm2'][l]) * (1 + sc2) + sh2
    x = x + gt2 * _peer(h, l, p)
    return x, s_fin


def setup_inputs(seed: int = 0) -> dict:
    key = jax.random.key(seed)
    ks = iter(jax.random.split(key, 64))
    f32 = jnp.float32

    def nrm(shape, scale):
        return jax.random.normal(next(ks), shape, f32) * scale

    def uni(shape, lo, hi):
        return jax.random.uniform(next(ks), shape, f32, lo, hi)

    D = D_MODEL
    return {
        'x_prompt': nrm((BATCH, SEQ, D), 1.0),
        'x_sample': nrm((DEC_BATCH, DEC_SEQ, D), 1.0),
        'state_rwkv': nrm((DEC_BATCH, N_A_LAYERS, 2, H_A, HD_A, HD_A), 0.5),
        'c': nrm((DEC_BATCH, D), 1.0),
        'c_ctx': nrm((D,), 1.0),
        'w_mod': nrm((DEPTH, D, 6 * D), 0.5 * D ** -0.5),
        'b_mod': nrm((DEPTH, 6 * D), 0.02),
        'norm1': 1.0 + nrm((DEPTH, D), 0.02),
        'norm2': 1.0 + nrm((DEPTH, D), 0.02),
        'norm_f': 1.0 + nrm((D,), 0.02),
        'w_in_ab': nrm((N_A_LAYERS, D, AB_COLS), D ** -0.5),
        'mu_a': uni((N_A_LAYERS, A_COLS), 0.0, 1.0),
        'w0': nrm((N_A_LAYERS, 2, A_W), 0.5),
        'w2': nrm((N_A_LAYERS, 2, W_LORA, A_W), 0.5 * W_LORA ** -0.5),
        'a0': nrm((N_A_LAYERS, 2, A_W), 0.3),
        'a2': nrm((N_A_LAYERS, 2, A_LORA, A_W), 0.5 * A_LORA ** -0.5),
        'g2': nrm((N_A_LAYERS, G_LORA, A_W), G_LORA ** -0.5),
        'k_k': 0.85 + nrm((N_A_LAYERS, A_W), 0.05),
        'k_a': 1.0 + nrm((N_A_LAYERS, A_W), 0.05),
        'r_k': nrm((N_A_LAYERS, A_W), 0.1),
        'lnx_g': 1.0 + nrm((N_A_LAYERS, A_W), 0.02),
        'lnx_b': nrm((N_A_LAYERS, A_W), 0.02),
        'lnv_g': 1.0 + nrm((N_A_LAYERS, B_W), 0.02),
        'lnv_b': nrm((N_A_LAYERS, B_W), 0.02),
        'w_s': nrm((N_A_LAYERS, H_B, CHUNK, CHUNK), CHUNK ** -0.5),
        'b_s': 1.0 + nrm((N_A_LAYERS, H_B, CHUNK), 0.1),
        'w_out_ab': nrm((N_A_LAYERS, MIX_W, D), MIX_W ** -0.5),
        'w_pw1': nrm((N_C_LAYERS, D, 2 * D), D ** -0.5),
        'b_pw1': nrm((N_C_LAYERS, 2 * D), 0.02),
        'conv_k': nrm((N_C_LAYERS, CONV_W, D), CONV_W ** -0.5),
        'conv_b': nrm((N_C_LAYERS, D), 0.02),
        'lnc_g': 1.0 + nrm((N_C_LAYERS, D), 0.02),
        'lnc_b': nrm((N_C_LAYERS, D), 0.02),
        'w_pw2': nrm((N_C_LAYERS, D, D), D ** -0.5),
        'b_pw2': nrm((N_C_LAYERS, D), 0.02),
        'w_query': nrm((DEPTH, D, PEER_HEADS * D_QUERY), D ** -0.5),
        'sub_keys': nrm((DEPTH, 2, N_KEYS, D_HALF), D_HALF ** -0.5),
        'expert_u': nrm((DEPTH, N_EXPERTS, D), D ** -0.5),
        'expert_v': nrm((DEPTH, N_EXPERTS, D), 0.5),
    }


def reference(x_prompt, x_sample, state_rwkv, c, c_ctx, w_mod, b_mod, norm1, norm2, norm_f,
              w_in_ab, mu_a, w0, w2, a0, a2, g2, k_k, k_a, r_k, lnx_g, lnx_b, lnv_g, lnv_b, w_s, b_s, w_out_ab,
              w_pw1, b_pw1, conv_k, conv_b, lnc_g, lnc_b, w_pw2, b_pw2,
              w_query, sub_keys, expert_u, expert_v):
    p = dict(norm1=norm1, norm2=norm2, w_in_ab=w_in_ab, mu_a=mu_a, w0=w0, w2=w2, a0=a0, a2=a2, g2=g2,
             k_k=k_k, k_a=k_a, r_k=r_k, lnx_g=lnx_g, lnx_b=lnx_b, lnv_g=lnv_g, lnv_b=lnv_b, w_s=w_s, b_s=b_s,
             w_out_ab=w_out_ab, w_pw1=w_pw1, b_pw1=b_pw1, conv_k=conv_k, conv_b=conv_b, lnc_g=lnc_g,
             lnc_b=lnc_b, w_pw2=w_pw2, b_pw2=b_pw2, w_query=w_query, sub_keys=sub_keys,
             expert_u=expert_u, expert_v=expert_v)

    xc = x_prompt
    n_ctx, ctx_len = x_prompt.shape[0], x_prompt.shape[1]
    zero_state = jnp.zeros((n_ctx, 2, H_A, HD_A, HD_A), jnp.float32)
    ctx_states = []
    for l in range(DEPTH):
        mod = _modulation(c_ctx[None, :], w_mod[l], b_mod[l])
        xc, s_fin = _layer(xc, mod, l, p, zero_state, ctx_len)
        if s_fin is not None:
            ctx_states.append(s_fin)
    y_prompt = _rmsnorm(xc, norm_f)
    new_state_rwkv = jnp.stack(ctx_states, axis=1).astype(x_prompt.dtype)

    rows = x_sample.shape[1] // GRID_W
    xs = x_sample
    for l in range(DEPTH):
        mod = _modulation(c, w_mod[l], b_mod[l])
        s0 = state_rwkv[:, l // 2] if l % 2 == 0 else None
        xs, _ = _layer(xs, mod, l, p, s0, x_sample.shape[1] // rows)
    y_sample = _rmsnorm(xs, norm_f)

    return (y_prompt, y_sample, new_state_rwkv)
```

```python
import functools
import math

import jax
import jax.numpy as jnp
from jax import lax
from jax.experimental import pallas as pl
from jax.experimental.pallas import tpu as pltpu

F32 = jnp.float32
BF16 = jnp.bfloat16

D = 1024
HD = 64
CH = 64
A_W = 512
NPAIR = A_W // 128
R_OFF, K_OFF, V_OFF = 0, 512, 1024
WLO_OFF, ALO_OFF, GLO_OFF, A_COLS = 1536, 1664, 1792, 1920
B_W = 512
GD = 128
CONV_W = 31
CONV_HALF = 15
CONV_PAD = 16
N_KEYS = 128
TOPK = 16
PEER_HEADS = 8
EPS = 1e-6
LN_EPS = 1e-5
GN_EPS = 64e-5
VMEM_LIMIT = 56 * 1024 * 1024


def _cparams(sem):
    return pltpu.CompilerParams(dimension_semantics=sem, vmem_limit_bytes=VMEM_LIMIT)


def _dot(a, b):
    return jnp.dot(a, b, preferred_element_type=F32)


def _dot_tb(a, b):
    return lax.dot_general(a, b, (((1,), (1,)), ((), ())), preferred_element_type=F32)


def _dot_ta(a, b):
    return lax.dot_general(a, b, (((0,), (0,)), ((), ())), preferred_element_type=F32)


def _split2(x):
    hi = x.astype(BF16)
    lo = (x - hi.astype(F32)).astype(BF16)
    return hi, lo


def _split3(x):
    hi = x.astype(BF16)
    r1 = x - hi.astype(F32)
    mid = r1.astype(BF16)
    lo = (r1 - mid.astype(F32)).astype(BF16)
    return hi, mid, lo


def _mm3(a, b, dot=_dot):
    ah, al = _split2(a)
    bh, bl = _split2(b)
    return dot(ah, bh) + dot(ah, bl) + dot(al, bh)


def _mm_exact_lhs(m_bf16, x):
    hi, mid, lo = _split3(x)
    return _dot(m_bf16, hi) + _dot(m_bf16, mid) + _dot(m_bf16, lo)


def _mm_exact_rhs(x, m_bf16):
    hi, mid, lo = _split3(x)
    return _dot(hi, m_bf16) + _dot(mid, m_bf16) + _dot(lo, m_bf16)


def _mod_part(mod_row, j):
    return mod_row[:, j * D:(j + 1) * D]


def _norm_mod(x, g, mod_row, j_shift, j_scale):
    ms = jnp.mean(x * x, axis=-1, keepdims=True)
    y = x * lax.rsqrt(ms + EPS) * g
    return y * (1.0 + _mod_part(mod_row, j_scale)) + _mod_part(mod_row, j_shift)


def _mod_row_map(n_ctx_tokens, seq_tokens, tm):
    ncb = n_ctx_tokens // tm
    per = seq_tokens // tm

    def imap(i, *_):
        return (jnp.where(i < ncb, 0, 1 + (i - ncb) // per), 0, 0)
    return imap


def _mod_kernel(c_ref, w_ref, b_ref, o_ref):
    c = c_ref[...]
    s = c * jax.nn.sigmoid(c)
    o_ref[0] = _mm3(s, w_ref[0]) + b_ref[0]


def _modulation(cvec8, w_mod, b_mod):
    depth, _, n6 = w_mod.shape
    tn = 1536
    return pl.pallas_call(
        _mod_kernel,
        out_shape=jax.ShapeDtypeStruct((depth, 8, n6), F32),
        grid=(depth, n6 // tn),
        in_specs=[pl.BlockSpec((8, D), lambda l, j: (0, 0)),
                  pl.BlockSpec((1, D, tn), lambda l, j: (l, 0, j)),
                  pl.BlockSpec((1, 1, tn), lambda l, j: (l, 0, j))],
        out_specs=pl.BlockSpec((1, 8, tn), lambda l, j: (l, 0, j)),
        compiler_params=_cparams(("arbitrary", "arbitrary")),
        name="modulation",
    )(cvec8, w_mod, b_mod.reshape(depth, 1, n6))


def _inproj_kernel(x_ref, g_ref, mod_ref, wa_ref, wb_ref, za_ref, ub_ref):
    h = _norm_mod(x_ref[...], g_ref[...], mod_ref[0], 0, 1).astype(BF16)
    za_ref[...] = _dot(h, wa_ref[...])
    ub_ref[...] = _dot(h, wb_ref[...])


def _inproj(x, g, mod, wa, wb, rowmap, tm):
    n = x.shape[0]
    return pl.pallas_call(
        _inproj_kernel,
        out_shape=(jax.ShapeDtypeStruct((n, A_COLS), F32), jax.ShapeDtypeStruct((n, 2 * B_W), F32)),
        grid=(n // tm,),
        in_specs=[pl.BlockSpec((tm, D), lambda i: (i, 0)),
                  pl.BlockSpec((1, D), lambda i: (0, 0)),
                  pl.BlockSpec((1, 1, 6 * D), rowmap),
                  pl.BlockSpec((D, A_COLS), lambda i: (0, 0)),
                  pl.BlockSpec((D, 2 * B_W), lambda i: (0, 0))],
        out_specs=(pl.BlockSpec((tm, A_COLS), lambda i: (i, 0)),
                   pl.BlockSpec((tm, 2 * B_W), lambda i: (i, 0))),
        compiler_params=_cparams(("arbitrary",)),
        name="inproj",
    )(x, g, mod, wa, wb)


def _softplus(x):
    return jnp.maximum(x, 0.0) + jnp.log1p(jnp.exp(-jnp.abs(x)))


def _rwkv_kernel(za_ref, s0_ref, mu_ref, w0_ref, w2_ref, a0_ref, a2_ref, g2_ref, kk_ref, ka_ref,
                 rk_ref, lng_ref, lnb_ref, hsum_ref, ya_ref, sfin_ref,
                 prep_ref, gam_ref, st_ref, y_ref, bon_ref, gate_ref):
    t_len = za_ref.shape[0]
    nc = t_len // CH
    mu = mu_ref[...]
    hsum = hsum_ref[...]
    row_c = lax.broadcasted_iota(jnp.int32, (CH, A_COLS), 0)
    ri = lax.broadcasted_iota(jnp.int32, (CH, CH), 0)
    ci = lax.broadcasted_iota(jnp.int32, (CH, CH), 1)
    tri_f = (ci <= ri).astype(BF16)
    tri_b = (ci >= ri).astype(BF16)
    r2 = lax.broadcasted_iota(jnp.int32, (2 * CH, 2 * CH), 0)
    c2 = lax.broadcasted_iota(jnp.int32, (2 * CH, 2 * CH), 1)
    r2m = jnp.bitwise_and(r2, CH - 1)
    c2m = jnp.bitwise_and(c2, CH - 1)
    eye2 = (r2 == c2).astype(F32)
    lane_lo = lax.broadcasted_iota(jnp.int32, (CH, 2 * HD), 1) < HD

    st_ref[...] = s0_ref[0]

    def stack(x):
        return jnp.concatenate([jnp.where(lane_lo, x, 0.0), jnp.where(lane_lo, 0.0, x)], axis=0)

    def mixed_chunk(c):
        start = pl.multiple_of(c * CH, CH)
        zc = za_ref[pl.ds(start, CH), :]
        p0 = pl.multiple_of(jnp.maximum(start - 8, 0), 8)
        n0 = pl.multiple_of(jnp.minimum(start + CH, t_len - 8), 8)
        prow = za_ref[pl.ds(p0, 8), :][7:8, :] * (start > 0).astype(F32)
        nrow = za_ref[pl.ds(n0, 8), :][0:1, :] * (start + CH < t_len).astype(F32)
        prev = jnp.where(row_c == 0, prow, pltpu.roll(zc, 1, 0))
        nxt = jnp.where(row_c == CH - 1, nrow, pltpu.roll(zc, CH - 1, 0))
        return zc + mu * (0.5 * (prev + nxt) - zc), start

    def prep(c, d):
        zm, start = mixed_chunk(c)
        r = zm[:, R_OFF:R_OFF + A_W]
        k = zm[:, K_OFF:K_OFF + A_W]
        v = zm[:, V_OFF:V_OFF + A_W]
        wlo2 = zm[:, WLO_OFF:WLO_OFF + 128]
        alo2 = zm[:, ALO_OFF:ALO_OFF + 128]
        kkr = k * kk_ref[...]
        n2 = _mm_exact_rhs(kkr * kkr, hsum)
        kk = kkr / jnp.maximum(jnp.sqrt(n2), 1e-12)
        wl = w0_ref[d:d + 1, :] + _mm3(jnp.tanh(wlo2), w2_ref[d])
        ld = -jnp.exp(-_softplus(-wl) - 0.5)
        ag = jax.nn.sigmoid(a0_ref[d:d + 1, :] + _mm3(alo2, a2_ref[d]))
        kd = k * (1.0 + (ag - 1.0) * ka_ref[...])
        bv = kk * ag
        cum = _mm_exact_lhs(tri_f if d == 0 else tri_b, ld)
        gam = jnp.exp(cum)
        gam_ex = jnp.exp(cum - ld)
        igam = jnp.exp(-cum)
        ops = (-kk * gam_ex, r * gam, bv * igam, kd * igam, v)
        for j, o in enumerate(ops):
            for p in range(NPAIR):
                prep_ref[d, j, p] = o[:, 128 * p:128 * (p + 1)]
        gtot = gam[CH - 1:CH, :] if d == 0 else gam[0:1, :]
        for p in range(NPAIR):
            gam_ref[d, p] = jnp.broadcast_to(gtot[:, 128 * p:128 * (p + 1)], (8, 128))
        bon_ref[d, pl.ds(start, CH), :] = _mm_exact_rhs(r * kd * rk_ref[...], hsum) * v
        if d == 0:
            glo = zm[:, GLO_OFF:GLO_OFF + 128]
            gate_ref[pl.ds(start, CH), :] = _mm3(jax.nn.sigmoid(glo), g2_ref[...])

    def pair_body(q, i):
        d = q // NPAIR
        p = q % NPAIR
        fwd = d == 0
        c = jnp.where(fwd, i, nc - 1 - i)
        start = pl.multiple_of(c * CH, CH)
        a_s = stack(prep_ref[d, 0, p])
        r_s = stack(prep_ref[d, 1, p])
        b_s = stack(prep_ref[d, 2, p])
        k_s = stack(prep_ref[d, 3, p])
        v_s = stack(prep_ref[d, 4, p])
        order = (r2m - c2m) * jnp.where(fwd, 1, -1)
        strict = order > 0
        incl = order >= 0
        sab = jnp.where(strict, _mm3(a_s, b_s, _dot_tb), 0.0)
        sak = jnp.where(strict, _mm3(a_s, k_s, _dot_tb), 0.0)
        nrb = jnp.where(incl, _mm3(r_s, b_s, _dot_tb), 0.0)
        nrk = jnp.where(incl, _mm3(r_s, k_s, _dot_tb), 0.0)
        tinv = eye2 + sab
        pw = sab
        for _ in range(int(math.log2(CH)) - 1):
            pw = _mm3(pw, pw)
            tinv = tinv + _mm3(tinv, pw)
        at = _mm3(tinv, a_s)
        u0 = _mm3(tinv, _mm3(sak, v_s))
        rt = r_s + _mm3(nrb, at)
        y0 = _mm3(nrb, u0) + _mm3(nrk, v_s)
        gt = gam_ref[d, p][0:1, :]
        g_t = (eye2 + _mm3(at, b_s, _dot_ta)) * gt
        h_t = (_mm3(u0, b_s, _dot_ta) + _mm3(v_s, k_s, _dot_ta)) * gt
        s_prev = st_ref[q]
        ys = _mm3(rt, s_prev, _dot_tb) + y0
        y_ref[d, p, pl.ds(start, CH), :] = ys[:CH, :] + ys[CH:, :]
        st_ref[q] = _mm3(s_prev, g_t) + h_t
        return i

    def chunk_body(i, carry):
        prep(i, 0)
        prep(nc - 1 - i, 1)
        lax.fori_loop(0, 2 * NPAIR, pair_body, i)
        return carry

    lax.fori_loop(0, nc, chunk_body, 0)

    def out_body(c, carry):
        start = pl.multiple_of(c * CH, CH)
        rows = pl.ds(start, CH)
        ysum = jnp.concatenate([y_ref[0, p, rows, :] + y_ref[1, p, rows, :] for p in range(NPAIR)], axis=1)
        mean = _mm_exact_rhs(ysum, hsum) * (1.0 / HD)
        dev = ysum - mean
        var = _mm_exact_rhs(dev * dev, hsum) * (1.0 / HD)
        gn = dev * lax.rsqrt(var + GN_EPS) * lng_ref[...] + lnb_ref[...]
        bonus = bon_ref[0, rows, :] + bon_ref[1, rows, :]
        ya_ref[rows, :] = ((gn + bonus) * gate_ref[rows, :]).astype(ya_ref.dtype)
        return carry

    lax.fori_loop(0, nc, out_body, 0)
    sfin_ref[0] = st_ref[...]


def _rwkv(za, s0bd, t_len, first_token, params):
    n_seq = s0bd.shape[0]
    blk0 = first_token // t_len
    nq = 2 * NPAIR
    full = lambda a: pl.BlockSpec(a.shape, lambda s, _n=a.ndim: (0,) * _n)
    return pl.pallas_call(
        _rwkv_kernel,
        out_shape=(jax.ShapeDtypeStruct((n_seq * t_len, A_W), BF16),
                   jax.ShapeDtypeStruct((n_seq, nq, 128, 128), F32)),
        grid=(n_seq,),
        in_specs=[pl.BlockSpec((t_len, A_COLS), lambda s: (s + blk0, 0)),
                  pl.BlockSpec((1, nq, 128, 128), lambda s: (s, 0, 0, 0))] + [full(a) for a in params],
        out_specs=(pl.BlockSpec((t_len, A_W), lambda s: (s, 0)),
                   pl.BlockSpec((1, nq, 128, 128), lambda s: (s, 0, 0, 0))),
        scratch_shapes=[pltpu.VMEM((2, 5, NPAIR, CH, 128), F32),
                        pltpu.VMEM((2, NPAIR, 8, 128), F32),
                        pltpu.VMEM((nq, 128, 128), F32),
                        pltpu.VMEM((2, NPAIR, t_len, 128), F32),
                        pltpu.VMEM((2, t_len, A_W), F32),
                        pltpu.VMEM((t_len, A_W), F32)],
        compiler_params=_cparams(("arbitrary",)),
        name="rwkv_t%d" % t_len,
    )(za, s0bd, *params)


def _states_to_blockdiag(s):
    n = s.shape[0]
    s = s.reshape(n, 2, NPAIR, 2, HD, HD)
    z = jnp.zeros_like(s[:, :, :, 0])
    top = jnp.concatenate([s[:, :, :, 0], z], axis=-1)
    bot = jnp.concatenate([z, s[:, :, :, 1]], axis=-1)
    return jnp.concatenate([top, bot], axis=-2).reshape(n, 2 * NPAIR, 128, 128)


def _blockdiag_to_states(b):
    n = b.shape[0]
    b = b.reshape(n, 2, NPAIR, 128, 128)
    s = jnp.stack([b[..., :HD, :HD], b[..., HD:, HD:]], axis=3)
    return s.reshape(n, 2, 2 * NPAIR, HD, HD)


def _post_kernel(x_ref, ya_ref, ub_ref, mod_ref, lvg_ref, lvb_ref, ws_ref, bs_ref, wo_ref, o_ref):
    tm = x_ref.shape[0]
    ub = ub_ref[...]
    parts = []
    for g in range(B_W // GD):
        u = ub[:, GD * g:GD * (g + 1)]
        vb = ub[:, B_W + GD * g:B_W + GD * (g + 1)]
        m = jnp.mean(vb, axis=-1, keepdims=True)
        dv = vb - m
        var = jnp.mean(dv * dv, axis=-1, keepdims=True)
        vn = dv * lax.rsqrt(var + LN_EPS) * lvg_ref[:, GD * g:GD * (g + 1)] + lvb_ref[:, GD * g:GD * (g + 1)]
        sp = jnp.concatenate(
            [_mm3(ws_ref[g], vn[GD * c:GD * (c + 1), :]) + bs_ref[g] for c in range(tm // GD)], axis=0)
        parts.append((u * sp).astype(BF16))
    y = jnp.concatenate([ya_ref[...]] + parts, axis=1)
    o_ref[...] = x_ref[...] + _mod_part(mod_ref[0], 2) * _dot(y, wo_ref[...])


def _post(x, ya, ub, mod, lvg, lvb, ws, bsb, wo, rowmap, tm):
    n = x.shape[0]
    c2 = lambda i: (0, 0)
    c3 = lambda i: (0, 0, 0)
    return pl.pallas_call(
        _post_kernel,
        out_shape=jax.ShapeDtypeStruct((n, D), F32),
        grid=(n // tm,),
        in_specs=[pl.BlockSpec((tm, D), lambda i: (i, 0)),
                  pl.BlockSpec((tm, A_W), lambda i: (i, 0)),
                  pl.BlockSpec((tm, 2 * B_W), lambda i: (i, 0)),
                  pl.BlockSpec((1, 1, 6 * D), rowmap),
                  pl.BlockSpec((1, B_W), c2), pl.BlockSpec((1, B_W), c2),
                  pl.BlockSpec(ws.shape, c3), pl.BlockSpec(bsb.shape, c3),
                  pl.BlockSpec((D, D), c2)],
        out_specs=pl.BlockSpec((tm, D), lambda i: (i, 0)),
        compiler_params=_cparams(("arbitrary",)),
        name="gmlp_outproj",
    )(x, ya, ub, mod, lvg, lvb, ws, bsb, wo)


def _query_kernel(x_ref, g_ref, mod_ref, wq_ref, hq_ref, q_ref):
    h = _norm_mod(x_ref[...], g_ref[...], mod_ref[0], 3, 4).astype(BF16)
    hq_ref[...] = h
    q_ref[...] = _dot(h, wq_ref[...])


def _query(x, g, mod, wq, rowmap, tm):
    n = x.shape[0]
    nq = wq.shape[1]
    return pl.pallas_call(
        _query_kernel,
        out_shape=(jax.ShapeDtypeStruct((n, D), BF16), jax.ShapeDtypeStruct((n, nq), F32)),
        grid=(n // tm,),
        in_specs=[pl.BlockSpec((tm, D), lambda i: (i, 0)),
                  pl.BlockSpec((1, D), lambda i: (0, 0)),
                  pl.BlockSpec((1, 1, 6 * D), rowmap),
                  pl.BlockSpec((D, nq), lambda i: (0, 0))],
        out_specs=(pl.BlockSpec((tm, D), lambda i: (i, 0)), pl.BlockSpec((tm, nq), lambda i: (i, 0))),
        compiler_params=_cparams(("arbitrary",)),
        name="peer_query",
    )(x, g, mod, wq)


def _extract_top(s, rowid, n_rows, count):
    rank = jnp.full(s.shape, float(count), F32)
    vals = []
    for r in range(count):
        m = jnp.max(s, axis=0, keepdims=True)
        idx = jnp.min(jnp.where(s == m, rowid, float(n_rows)), axis=0, keepdims=True)
        hit = rowid == idx
        rank = jnp.where(hit, float(r), rank)
        s = jnp.where(hit, -jnp.inf, s)
        vals.append(m)
    return rank, vals


_CAND_Q = [TOPK // (p + 1) for p in range(TOPK)]


def _topk_kernel(q_ref, sk_ref, cnt1_ref, g1_ref, rank2_ref, g2_ref):
    tb = q_ref.shape[0]
    q = q_ref[...]
    s1 = _mm3(sk_ref[0], q[:, :N_KEYS], _dot_tb)
    s2 = _mm3(sk_ref[1], q[:, N_KEYS:], _dot_tb)
    rowid = lax.broadcasted_iota(jnp.int32, (N_KEYS, tb), 0).astype(F32)
    rank1, a = _extract_top(s1, rowid, N_KEYS, TOPK)
    rank2, b = _extract_top(s2, rowid, N_KEYS, TOPK)
    bmat = jnp.concatenate(b, axis=0)
    amat = jnp.concatenate(a, axis=0)
    neg = -jnp.inf
    row8 = lax.broadcasted_iota(jnp.int32, (8, tb), 0)
    blocks = [a[0] + bmat]
    for p in range(1, 8):
        blocks.append(jnp.where(row8 < _CAND_Q[p], a[p] + bmat[:8, :], neg))
    blocks.append(amat[8:, :] + b[0])
    cand = jnp.concatenate(blocks, axis=0)
    n_cand = cand.shape[0]
    crow = lax.broadcasted_iota(jnp.int32, (n_cand, tb), 0).astype(F32)
    crank, _ = _extract_top(cand, crow, n_cand, TOPK)
    sel = crank < float(TOPK)
    top = a[0] + b[0]
    e = jnp.where(sel, jnp.exp(cand - top), 0.0)
    z = jnp.sum(e, axis=0, keepdims=True)
    self32 = sel.astype(F32)
    cnt = [jnp.sum(self32[0:16], axis=0, keepdims=True)]
    for p in range(1, 8):
        cnt.append(jnp.sum(self32[8 + 8 * p:16 + 8 * p], axis=0, keepdims=True))
    for p in range(8, 16):
        cnt.append(self32[72 + (p - 8):73 + (p - 8)])
    cnt1 = jnp.zeros((N_KEYS, tb), F32)
    for p in range(TOPK):
        cnt1 = jnp.where(rank1 == float(p), cnt[p], cnt1)
    cnt1_ref[0] = cnt1
    g1_ref[0] = jnp.exp(s1 - a[0]) / z
    rank2_ref[0] = rank2
    g2_ref[0] = jnp.exp(s2 - b[0])


def _topk(q, sk, tb):
    n = q.shape[0]
    shp = jax.ShapeDtypeStruct((PEER_HEADS, N_KEYS, n), F32)
    ospec = pl.BlockSpec((1, N_KEYS, tb), lambda i, h: (h, 0, i))
    return pl.pallas_call(
        _topk_kernel,
        out_shape=(shp, shp, shp, shp),
        grid=(n // tb, PEER_HEADS),
        in_specs=[pl.BlockSpec((tb, 2 * N_KEYS), lambda i, h: (i, h)),
                  pl.BlockSpec((2, N_KEYS, N_KEYS), lambda i, h: (0, 0, 0))],
        out_specs=(ospec, ospec, ospec, ospec),
        compiler_params=_cparams(("arbitrary", "arbitrary")),
        name="peer_topk",
    )(q, sk)


I1_TILE = 8


def _experts_kernel(hq_ref, eu_ref, evt_ref, cnt1_ref, g1_ref, rank2_ref, g2_ref, x_ref, mod_ref,
                    o_ref, acc_ref, w_ref):
    j = pl.program_id(1)
    tm = hq_ref.shape[0]

    @pl.when(j == 0)
    def _():
        acc_ref[...] = jnp.zeros_like(acc_ref)

    for ii in range(I1_TILE):
        for rg in range(N_KEYS // 8):
            rows = slice(8 * rg, 8 * (rg + 1))
            w8 = jnp.zeros((8, tm), F32)
            for h in range(PEER_HEADS):
                c = cnt1_ref[h, ii:ii + 1, :]
                g = g1_ref[h, ii:ii + 1, :]
                w8 = w8 + jnp.where(rank2_ref[h, rows, :] < c, g2_ref[h, rows, :] * g, 0.0)
            w_ref[N_KEYS * ii + 8 * rg:N_KEYS * ii + 8 * (rg + 1), :] = w8

    act = _dot_tb(eu_ref[...], hq_ref[...])
    gel = act * (lax.erf(act * (1.0 / math.sqrt(2.0))) + 1.0) * 0.5
    acc_ref[...] += _dot(evt_ref[...], (w_ref[...] * gel).astype(BF16))

    @pl.when(j == pl.num_programs(1) - 1)
    def _():
        o_ref[...] = x_ref[...] + _mod_part(mod_ref[0], 5) * acc_ref[...].T


def _experts(hq, eu, evt, cnt1, g1, rank2, g2, x, mod, rowmap, tm):
    n = x.shape[0]
    n_exp = eu.shape[0]
    te = I1_TILE * N_KEYS
    return pl.pallas_call(
        _experts_kernel,
        out_shape=jax.ShapeDtypeStruct((n, D), F32),
        grid=(n // tm, n_exp // te),
        in_specs=[pl.BlockSpec((tm, D), lambda i, j: (i, 0)),
                  pl.BlockSpec((te, D), lambda i, j: (j, 0)),
                  pl.BlockSpec((D, te), lambda i, j: (0, j)),
                  pl.BlockSpec((PEER_HEADS, I1_TILE, tm), lambda i, j: (0, j, i)),
                  pl.BlockSpec((PEER_HEADS, I1_TILE, tm), lambda i, j: (0, j, i)),
                  pl.BlockSpec((PEER_HEADS, N_KEYS, tm), lambda i, j: (0, 0, i)),
                  pl.BlockSpec((PEER_HEADS, N_KEYS, tm), lambda i, j: (0, 0, i)),
                  pl.BlockSpec((tm, D), lambda i, j: (i, 0)),
                  pl.BlockSpec((1, 1, 6 * D), lambda i, j: rowmap(i))],
        out_specs=pl.BlockSpec((tm, D), lambda i, j: (i, 0)),
        scratch_shapes=[pltpu.VMEM((D, tm), F32), pltpu.VMEM((te, tm), F32)],
        compiler_params=_cparams(("arbitrary", "arbitrary")),
        name="peer_experts",
    )(hq, eu, evt, cnt1, g1, rank2, g2, x, mod)


def _pw1_kernel(x_ref, g_ref, mod_ref, wa_ref, wb_ref, ba_ref, bb_ref, o_ref):
    h = _norm_mod(x_ref[...], g_ref[...], mod_ref[0], 0, 1).astype(BF16)
    a = _dot(h, wa_ref[...]) + ba_ref[...]
    b = _dot(h, wb_ref[...]) + bb_ref[...]
    o_ref[...] = a * jax.nn.sigmoid(b)


def _pw1(x, g, mod, wa, wb, ba, bb, rowmap, tm):
    n = x.shape[0]
    c2 = lambda i: (0, 0)
    return pl.pallas_call(
        _pw1_kernel,
        out_shape=jax.ShapeDtypeStruct((n, D), F32),
        grid=(n // tm,),
        in_specs=[pl.BlockSpec((tm, D), lambda i: (i, 0)),
                  pl.BlockSpec((1, D), c2),
                  pl.BlockSpec((1, 1, 6 * D), rowmap),
                  pl.BlockSpec((D, D), c2), pl.BlockSpec((D, D), c2),
                  pl.BlockSpec((1, D), c2), pl.BlockSpec((1, D), c2)],
        out_specs=pl.BlockSpec((tm, D), lambda i: (i, 0)),
        compiler_params=_cparams(("arbitrary",)),
        name="conv_pw1_glu",
    )(x, g, mod, wa, wb, ba, bb)


def _conv_kernel(n_ctx_blocks, seg_ctx, seg_lat,
                 z_ref, x_ref, mod_ref, ck_ref, cb_ref, lg_ref, lb_ref, w2_ref, b2_ref, o_ref, pad_ref):
    tm = z_ref.shape[0]
    i = pl.program_id(0)
    seg = jnp.where(i < n_ctx_blocks, seg_ctx, seg_lat)
    zeros = jnp.zeros((CONV_PAD, D), F32)
    pad_ref[0:CONV_PAD, :] = zeros
    pad_ref[CONV_PAD + tm:, :] = zeros
    pad_ref[CONV_PAD:CONV_PAD + tm, :] = z_ref[...]
    pos = jnp.bitwise_and(lax.broadcasted_iota(jnp.int32, (tm, D), 0), seg - 1)
    acc = jnp.zeros((tm, D), F32)
    for k in range(CONV_W):
        off = k - CONV_HALF
        xs = pad_ref[CONV_PAD + off:CONV_PAD + off + tm, :]
        if off < 0:
            xs = jnp.where(pos >= -off, xs, 0.0)
        elif off > 0:
            xs = jnp.where(pos < seg - off, xs, 0.0)
        acc = acc + xs * ck_ref[k:k + 1, :]
    z = acc + cb_ref[...]
    m = jnp.mean(z, axis=-1, keepdims=True)
    dv = z - m
    var = jnp.mean(dv * dv, axis=-1, keepdims=True)
    zn = dv * lax.rsqrt(var + LN_EPS) * lg_ref[...] + lb_ref[...]
    act = (zn * jax.nn.sigmoid(zn)).astype(BF16)
    y = _dot(act, w2_ref[...]) + b2_ref[...]
    o_ref[...] = x_ref[...] + _mod_part(mod_ref[0], 2) * y


def _conv(z, x, mod, ck, cb, lg, lb, w2, b2, rowmap, tm, n_ctx_blocks, seg_ctx, seg_lat):
    n = x.shape[0]
    c2 = lambda i: (0, 0)
    return pl.pallas_call(
        functools.partial(_conv_kernel, n_ctx_blocks, seg_ctx, seg_lat),
        out_shape=jax.ShapeDtypeStruct((n, D), F32),
        grid=(n // tm,),
        in_specs=[pl.BlockSpec((tm, D), lambda i: (i, 0)),
                  pl.BlockSpec((tm, D), lambda i: (i, 0)),
                  pl.BlockSpec((1, 1, 6 * D), rowmap),
                  pl.BlockSpec(ck.shape, c2),
                  pl.BlockSpec((1, D), c2), pl.BlockSpec((1, D), c2), pl.BlockSpec((1, D), c2),
                  pl.BlockSpec((D, D), c2), pl.BlockSpec((1, D), c2)],
        out_specs=pl.BlockSpec((tm, D), lambda i: (i, 0)),
        scratch_shapes=[pltpu.VMEM((tm + 2 * CONV_PAD, D), F32)],
        compiler_params=_cparams(("arbitrary",)),
        name="conv_dw_ln_pw2",
    )(z, x, mod, ck, cb, lg, lb, w2, b2)


def _final_kernel(x_ref, g_ref, o_ref):
    x = x_ref[...]
    ms = jnp.mean(x * x, axis=-1, keepdims=True)
    o_ref[...] = x * lax.rsqrt(ms + EPS) * g_ref[...]


def _final_norm(x, g, tm):
    n = x.shape[0]
    return pl.pallas_call(
        _final_kernel,
        out_shape=jax.ShapeDtypeStruct((n, D), F32),
        grid=(n // tm,),
        in_specs=[pl.BlockSpec((tm, D), lambda i: (i, 0)), pl.BlockSpec((1, D), lambda i: (0, 0))],
        out_specs=pl.BlockSpec((tm, D), lambda i: (i, 0)),
        compiler_params=_cparams(("arbitrary",)),
        name="final_norm",
    )(x, g)


def _peer_layer(x, l, mod_l, norm2, w_query, sub_keys, expert_u, expert_v, rowmap_of):
    hq, q = _query(x, norm2[l][None, :], mod_l, w_query[l].astype(BF16), rowmap_of(512), 512)
    cnt1, g1, rank2, g2 = _topk(q, sub_keys[l], 512)
    eu = expert_u[l].astype(BF16)
    evt = expert_v[l].astype(BF16).T
    return _experts(hq, eu, evt, cnt1, g1, rank2, g2, x, mod_l, rowmap_of(512), 512)


def kernel(x_prompt, x_sample, state_rwkv, c, c_ctx, w_mod, b_mod, norm1, norm2, norm_f, w_in_ab, mu_a, w0, w2, a0, a2, g2, k_k, k_a, r_k, lnx_g, lnx_b, lnv_g, lnv_b, w_s, b_s, w_out_ab, w_pw1, b_pw1, conv_k, conv_b, lnc_g, lnc_b, w_pw2, b_pw2, w_query, sub_keys, expert_u, expert_v):
    n_ctx, ctx_len, _ = x_prompt.shape
    n_lat, lat_len, _ = x_sample.shape
    nc_tok = n_ctx * ctx_len
    grid_w = 64
    x = jnp.concatenate([x_prompt.reshape(nc_tok, D), x_sample.reshape(n_lat * lat_len, D)], axis=0)

    cvec8 = jnp.zeros((8, D), F32).at[0].set(c_ctx).at[1:1 + n_lat].set(c)
    mod = _modulation(cvec8, w_mod, b_mod)
    mods = [mod[l].reshape(8, 1, 6 * D) for l in range(mod.shape[0])]
    rowmap_of = lambda tm: _mod_row_map(nc_tok, lat_len, tm)

    w_in = w_in_ab[0]
    za, ub = _inproj(x, norm1[0][None, :], mods[0], w_in[:, :A_COLS].astype(BF16),
                     w_in[:, A_COLS:].astype(BF16), rowmap_of(512), 512)
    row = lambda a: a.reshape(1, -1)
    pad_lora = lambda w, off: jnp.zeros((2, 128, A_W), F32).at[0, :HD].set(w[0]).at[1, HD:].set(w[1])
    heads = jnp.arange(A_W) // HD
    hsum = (heads[:, None] == heads[None, :]).astype(BF16)
    rparams = (row(mu_a[0]), w0[0], pad_lora(w2[0], 0), a0[0], pad_lora(a2[0], 0), g2[0],
               row(k_k[0]), row(k_a[0]), row(r_k[0]), row(lnx_g[0]), row(lnx_b[0]), hsum)
    zero_state = jnp.zeros((n_ctx, 2 * NPAIR, 128, 128), F32)
    ya_c, sfin = _rwkv(za, zero_state, ctx_len, 0, rparams)
    ya_s, _ = _rwkv(za, _states_to_blockdiag(state_rwkv[:, 0]), lat_len, nc_tok, rparams)
    ya = jnp.concatenate([ya_c, ya_s], axis=0)
    bsb = jnp.broadcast_to(b_s[0][:, :, None], (b_s.shape[1], GD, GD))
    x = _post(x, ya, ub, mods[0], row(lnv_g[0]), row(lnv_b[0]), w_s[0], bsb,
              w_out_ab[0].astype(BF16), rowmap_of(512), 512)
    x = _peer_layer(x, 0, mods[0], norm2, w_query, sub_keys, expert_u, expert_v, rowmap_of)

    glu = _pw1(x, norm1[1][None, :], mods[1], w_pw1[0][:, :D].astype(BF16), w_pw1[0][:, D:].astype(BF16),
               row(b_pw1[0][:D]), row(b_pw1[0][D:]), rowmap_of(512), 512)
    ck = jnp.zeros((32, D), F32).at[:CONV_W].set(conv_k[0])
    x = _conv(glu, x, mods[1], ck, row(conv_b[0]), row(lnc_g[0]), row(lnc_b[0]),
              w_pw2[0].astype(BF16), row(b_pw2[0]), rowmap_of(256), 256,
              nc_tok // 256, ctx_len, grid_w)
    x = _peer_layer(x, 1, mods[1], norm2, w_query, sub_keys, expert_u, expert_v, rowmap_of)

    y = _final_norm(x, norm_f[None, :], 512)
    y_prompt = y[:nc_tok].reshape(n_ctx, ctx_len, D)
    y_sample = y[nc_tok:].reshape(n_lat, lat_len, D)
    new_state = _blockdiag_to_states(sfin)[:, None].astype(x_prompt.dtype)
    return (y_prompt, y_sample, new_state)
```

```python
import functools
import math

import jax
import jax.numpy as jnp
from jax import lax
from jax.experimental import pallas as pl
from jax.experimental.pallas import tpu as pltpu

F32 = jnp.float32
BF16 = jnp.bfloat16

D = 1024
HD = 64
CH = 64
A_W = 512
NPAIR = A_W // 128
R_OFF, K_OFF, V_OFF = 0, 512, 1024
WLO_OFF, ALO_OFF, GLO_OFF, A_COLS = 1536, 1664, 1792, 1920
B_W = 512
GD = 128
CONV_W = 31
CONV_HALF = 15
CONV_PAD = 16
N_KEYS = 128
TOPK = 16
PEER_HEADS = 8
EPS = 1e-6
LN_EPS = 1e-5
GN_EPS = 64e-5
VMEM_LIMIT = 56 * 1024 * 1024


def _cparams(sem):
    return pltpu.CompilerParams(dimension_semantics=sem, vmem_limit_bytes=VMEM_LIMIT)


def _dot(a, b):
    return jnp.dot(a, b, preferred_element_type=F32)


def _dot_tb(a, b):
    return lax.dot_general(a, b, (((1,), (1,)), ((), ())), preferred_element_type=F32)


def _dot_ta(a, b):
    return lax.dot_general(a, b, (((0,), (0,)), ((), ())), preferred_element_type=F32)


def _split2(x):
    hi = x.astype(BF16)
    lo = (x - hi.astype(F32)).astype(BF16)
    return hi, lo


def _split3(x):
    hi = x.astype(BF16)
    r1 = x - hi.astype(F32)
    mid = r1.astype(BF16)
    lo = (r1 - mid.astype(F32)).astype(BF16)
    return hi, mid, lo


def _mm3(a, b, dot=_dot):
    ah, al = _split2(a)
    bh, bl = _split2(b)
    return dot(ah, bh) + dot(ah, bl) + dot(al, bh)


def _stack4(p, axis, lhs):
    hi, lo = p
    return jnp.concatenate([hi, hi, lo, lo] if lhs else [hi, lo, hi, lo], axis=axis)


def _mm4(a_p, b_p):
    return _dot(_stack4(a_p, 1, True), _stack4(b_p, 0, False))


def _mm4_tb(a_p, b_p):
    return _dot_tb(_stack4(a_p, 1, True), _stack4(b_p, 1, False))


def _mm4_ta(a_p, b_p):
    return _dot_ta(_stack4(a_p, 0, True), _stack4(b_p, 0, False))


def _mm_exact_lhs(m_bf16, x):
    hi, mid, lo = _split3(x)
    return _dot(m_bf16, hi) + _dot(m_bf16, mid) + _dot(m_bf16, lo)


def _mm_exact_rhs(x, m_bf16):
    hi, mid, lo = _split3(x)
    return _dot(hi, m_bf16) + _dot(mid, m_bf16) + _dot(lo, m_bf16)


def _mod_part(mod_row, j):
    return mod_row[:, j * D:(j + 1) * D]


def _norm_mod(x, g, mod_row, j_shift, j_scale):
    ms = jnp.mean(x * x, axis=-1, keepdims=True)
    y = x * lax.rsqrt(ms + EPS) * g
    return y * (1.0 + _mod_part(mod_row, j_scale)) + _mod_part(mod_row, j_shift)


def _mod_row_map(n_ctx_tokens, seq_tokens, tm):
    ncb = n_ctx_tokens // tm
    per = seq_tokens // tm

    def imap(i, *_):
        return (jnp.where(i < ncb, 0, 1 + (i - ncb) // per), 0, 0)
    return imap


def _mod_kernel(c_ref, w_ref, b_ref, o_ref):
    c = c_ref[...]
    s = c * jax.nn.sigmoid(c)
    o_ref[0] = _mm3(s, w_ref[0]) + b_ref[0]


def _modulation(cvec8, w_mod, b_mod):
    depth, _, n6 = w_mod.shape
    tn = 1536
    return pl.pallas_call(
        _mod_kernel,
        out_shape=jax.ShapeDtypeStruct((depth, 8, n6), F32),
        grid=(depth, n6 // tn),
        in_specs=[pl.BlockSpec((8, D), lambda l, j: (0, 0)),
                  pl.BlockSpec((1, D, tn), lambda l, j: (l, 0, j)),
                  pl.BlockSpec((1, 1, tn), lambda l, j: (l, 0, j))],
        out_specs=pl.BlockSpec((1, 8, tn), lambda l, j: (l, 0, j)),
        compiler_params=_cparams(("arbitrary", "arbitrary")),
        name="modulation",
    )(cvec8, w_mod, b_mod.reshape(depth, 1, n6))


def _inproj_kernel(x_ref, g_ref, mod_ref, wa_ref, wb_ref, za_ref, ub_ref):
    h = _norm_mod(x_ref[...], g_ref[...], mod_ref[0], 0, 1).astype(BF16)
    za_ref[...] = _dot(h, wa_ref[...])
    ub_ref[...] = _dot(h, wb_ref[...])


def _inproj(x, g, mod, wa, wb, rowmap, tm):
    n = x.shape[0]
    return pl.pallas_call(
        _inproj_kernel,
        out_shape=(jax.ShapeDtypeStruct((n, A_COLS), F32), jax.ShapeDtypeStruct((n, 2 * B_W), F32)),
        grid=(n // tm,),
        in_specs=[pl.BlockSpec((tm, D), lambda i: (i, 0)),
                  pl.BlockSpec((1, D), lambda i: (0, 0)),
                  pl.BlockSpec((1, 1, 6 * D), rowmap),
                  pl.BlockSpec((D, A_COLS), lambda i: (0, 0)),
                  pl.BlockSpec((D, 2 * B_W), lambda i: (0, 0))],
        out_specs=(pl.BlockSpec((tm, A_COLS), lambda i: (i, 0)),
                   pl.BlockSpec((tm, 2 * B_W), lambda i: (i, 0))),
        compiler_params=_cparams(("arbitrary",)),
        name="inproj",
    )(x, g, mod, wa, wb)


def _softplus(x):
    return jnp.maximum(x, 0.0) + jnp.log1p(jnp.exp(-jnp.abs(x)))


def _rwkv_kernel(za_ref, s0_ref, mu_ref, w0_ref, w2_ref, a0_ref, a2_ref, g2_ref, kk_ref, ka_ref,
                 rk_ref, lng_ref, lnb_ref, hsum_ref, ya_ref, sfin_ref,
                 prep_ref, gam_ref, st_ref, y_ref, bon_ref, gate_ref):
    t_len = za_ref.shape[0]
    nc = t_len // CH
    mu = mu_ref[...]
    hsum = hsum_ref[...]
    row_c = lax.broadcasted_iota(jnp.int32, (CH, A_COLS), 0)
    ri = lax.broadcasted_iota(jnp.int32, (CH, CH), 0)
    ci = lax.broadcasted_iota(jnp.int32, (CH, CH), 1)
    tri_f = (ci <= ri).astype(BF16)
    tri_b = (ci >= ri).astype(BF16)
    r2 = lax.broadcasted_iota(jnp.int32, (2 * CH, 2 * CH), 0)
    c2 = lax.broadcasted_iota(jnp.int32, (2 * CH, 2 * CH), 1)
    r2m = jnp.bitwise_and(r2, CH - 1)
    c2m = jnp.bitwise_and(c2, CH - 1)
    eye2 = (r2 == c2).astype(F32)
    lane_lo = lax.broadcasted_iota(jnp.int32, (CH, 2 * HD), 1) < HD

    st_ref[...] = s0_ref[0]

    def stack(x):
        return jnp.concatenate([jnp.where(lane_lo, x, 0.0), jnp.where(lane_lo, 0.0, x)], axis=0)

    def mixed_chunk(c):
        start = pl.multiple_of(c * CH, CH)
        zc = za_ref[pl.ds(start, CH), :]
        p0 = pl.multiple_of(jnp.maximum(start - 8, 0), 8)
        n0 = pl.multiple_of(jnp.minimum(start + CH, t_len - 8), 8)
        prow = za_ref[pl.ds(p0, 8), :][7:8, :] * (start > 0).astype(F32)
        nrow = za_ref[pl.ds(n0, 8), :][0:1, :] * (start + CH < t_len).astype(F32)
        prev = jnp.where(row_c == 0, prow, pltpu.roll(zc, 1, 0))
        nxt = jnp.where(row_c == CH - 1, nrow, pltpu.roll(zc, CH - 1, 0))
        return zc + mu * (0.5 * (prev + nxt) - zc), start

    def prep(c, d):
        zm, start = mixed_chunk(c)
        r = zm[:, R_OFF:R_OFF + A_W]
        k = zm[:, K_OFF:K_OFF + A_W]
        v = zm[:, V_OFF:V_OFF + A_W]
        wlo2 = zm[:, WLO_OFF:WLO_OFF + 128]
        alo2 = zm[:, ALO_OFF:ALO_OFF + 128]
        kkr = k * kk_ref[...]
        n2 = _mm_exact_rhs(kkr * kkr, hsum)
        kk = kkr / jnp.maximum(jnp.sqrt(n2), 1e-12)
        wl = w0_ref[d:d + 1, :] + _mm3(jnp.tanh(wlo2), w2_ref[d])
        ld = -jnp.exp(-_softplus(-wl) - 0.5)
        ag = jax.nn.sigmoid(a0_ref[d:d + 1, :] + _mm3(alo2, a2_ref[d]))
        kd = k * (1.0 + (ag - 1.0) * ka_ref[...])
        bv = kk * ag
        cum = _mm_exact_lhs(tri_f if d == 0 else tri_b, ld)
        gam = jnp.exp(cum)
        gam_ex = jnp.exp(cum - ld)
        igam = jnp.exp(-cum)
        ops = (-kk * gam_ex, r * gam, bv * igam, kd * igam, v)
        for j, o in enumerate(ops):
            for p in range(NPAIR):
                prep_ref[d, j, p] = o[:, 128 * p:128 * (p + 1)]
        gtot = gam[CH - 1:CH, :] if d == 0 else gam[0:1, :]
        for p in range(NPAIR):
            gam_ref[d, p] = jnp.broadcast_to(gtot[:, 128 * p:128 * (p + 1)], (8, 128))
        bon_ref[d, pl.ds(start, CH), :] = _mm_exact_rhs(r * kd * rk_ref[...], hsum) * v
        if d == 0:
            glo = zm[:, GLO_OFF:GLO_OFF + 128]
            gate_ref[pl.ds(start, CH), :] = _mm3(jax.nn.sigmoid(glo), g2_ref[...])

    def pair_body(q, i):
        d = q // NPAIR
        p = q % NPAIR
        fwd = d == 0
        c = jnp.where(fwd, i, nc - 1 - i)
        start = pl.multiple_of(c * CH, CH)
        n2 = 2 * CH
        a_s = stack(prep_ref[d, 0, p])
        r_s = stack(prep_ref[d, 1, p])
        b_p = _split2(stack(prep_ref[d, 2, p]))
        k_p = _split2(stack(prep_ref[d, 3, p]))
        v_p = _split2(stack(prep_ref[d, 4, p]))
        order = (r2m - c2m) * jnp.where(fwd, 1, -1)
        strict = order > 0
        incl = order >= 0
        ar_p = _split2(jnp.concatenate([a_s, r_s], axis=0))
        bk_p = tuple(jnp.concatenate([x, y], axis=0) for x, y in zip(b_p, k_p))
        sc = _mm4_tb(ar_p, bk_p)
        sab = jnp.where(strict, sc[:n2, :n2], 0.0)
        sak = jnp.where(strict, sc[:n2, n2:], 0.0)
        nrb = jnp.where(incl, sc[n2:, :n2], 0.0)
        nrk = jnp.where(incl, sc[n2:, n2:], 0.0)
        pw_p = _split2(sab)
        pw = _mm4(pw_p, pw_p)
        tinv = eye2 + sab
        n_dbl = int(math.log2(CH))
        for kk in range(1, n_dbl):
            pw_p = _split2(pw)
            if kk < n_dbl - 1:
                both = _mm4(pw_p, _split2(jnp.concatenate([pw, tinv], axis=1)))
                pw = both[:, :n2]
                tinv = tinv + both[:, n2:]
            else:
                tinv = tinv + _mm4(pw_p, _split2(tinv))
        sakv = _mm4(_split2(sak), v_p)
        au = _mm4(_split2(tinv), _split2(jnp.concatenate([a_s, sakv], axis=1)))
        au_p = _split2(au)
        ry = _mm4(_split2(nrb), au_p)
        rt = r_s + ry[:, :n2]
        y0 = ry[:, n2:] + _mm4(_split2(nrk), v_p)
        gt = gam_ref[d, p][0:1, :]
        gh = _mm4_ta(au_p, b_p)
        g_t = (eye2 + gh[:n2, :]) * gt
        h_t = (gh[n2:, :] + _mm4_ta(v_p, k_p)) * gt
        s_p = _split2(st_ref[q])
        ys = _mm4_tb(_split2(rt), s_p) + y0
        y_ref[d, p, pl.ds(start, CH), :] = ys[:CH, :] + ys[CH:, :]
        st_ref[q] = _mm4(s_p, _split2(g_t)) + h_t
        return i

    def chunk_body(i, carry):
        prep(i, 0)
        prep(nc - 1 - i, 1)
        lax.fori_loop(0, 2 * NPAIR, pair_body, i)
        return carry

    lax.fori_loop(0, nc, chunk_body, 0)

    def out_body(c, carry):
        start = pl.multiple_of(c * CH, CH)
        rows = pl.ds(start, CH)
        ysum = jnp.concatenate([y_ref[0, p, rows, :] + y_ref[1, p, rows, :] for p in range(NPAIR)], axis=1)
        mean = _mm_exact_rhs(ysum, hsum) * (1.0 / HD)
        dev = ysum - mean
        var = _mm_exact_rhs(dev * dev, hsum) * (1.0 / HD)
        gn = dev * lax.rsqrt(var + GN_EPS) * lng_ref[...] + lnb_ref[...]
        bonus = bon_ref[0, rows, :] + bon_ref[1, rows, :]
        ya_ref[rows, :] = ((gn + bonus) * gate_ref[rows, :]).astype(ya_ref.dtype)
        return carry

    lax.fori_loop(0, nc, out_body, 0)
    sfin_ref[0] = st_ref[...]


def _rwkv(za, s0bd, t_len, first_token, params):
    n_seq = s0bd.shape[0]
    blk0 = first_token // t_len
    nq = 2 * NPAIR
    full = lambda a: pl.BlockSpec(a.shape, lambda s, _n=a.ndim: (0,) * _n)
    return pl.pallas_call(
        _rwkv_kernel,
        out_shape=(jax.ShapeDtypeStruct((n_seq * t_len, A_W), BF16),
                   jax.ShapeDtypeStruct((n_seq, nq, 128, 128), F32)),
        grid=(n_seq,),
        in_specs=[pl.BlockSpec((t_len, A_COLS), lambda s: (s + blk0, 0)),
                  pl.BlockSpec((1, nq, 128, 128), lambda s: (s, 0, 0, 0))] + [full(a) for a in params],
        out_specs=(pl.BlockSpec((t_len, A_W), lambda s: (s, 0)),
                   pl.BlockSpec((1, nq, 128, 128), lambda s: (s, 0, 0, 0))),
        scratch_shapes=[pltpu.VMEM((2, 5, NPAIR, CH, 128), F32),
                        pltpu.VMEM((2, NPAIR, 8, 128), F32),
                        pltpu.VMEM((nq, 128, 128), F32),
                        pltpu.VMEM((2, NPAIR, t_len, 128), F32),
                        pltpu.VMEM((2, t_len, A_W), F32),
                        pltpu.VMEM((t_len, A_W), F32)],
        compiler_params=_cparams(("arbitrary",)),
        name="rwkv_t%d" % t_len,
    )(za, s0bd, *params)


def _states_to_blockdiag(s):
    n = s.shape[0]
    s = s.reshape(n, 2, NPAIR, 2, HD, HD)
    z = jnp.zeros_like(s[:, :, :, 0])
    top = jnp.concatenate([s[:, :, :, 0], z], axis=-1)
    bot = jnp.concatenate([z, s[:, :, :, 1]], axis=-1)
    return jnp.concatenate([top, bot], axis=-2).reshape(n, 2 * NPAIR, 128, 128)


def _blockdiag_to_states(b):
    n = b.shape[0]
    b = b.reshape(n, 2, NPAIR, 128, 128)
    s = jnp.stack([b[..., :HD, :HD], b[..., HD:, HD:]], axis=3)
    return s.reshape(n, 2, 2 * NPAIR, HD, HD)


def _post_kernel(x_ref, ya_ref, ub_ref, mod_ref, lvg_ref, lvb_ref, ws_ref, bs_ref, wo_ref, o_ref):
    tm = x_ref.shape[0]
    ub = ub_ref[...]
    parts = []
    for g in range(B_W // GD):
        u = ub[:, GD * g:GD * (g + 1)]
        vb = ub[:, B_W + GD * g:B_W + GD * (g + 1)]
        m = jnp.mean(vb, axis=-1, keepdims=True)
        dv = vb - m
        var = jnp.mean(dv * dv, axis=-1, keepdims=True)
        vn = dv * lax.rsqrt(var + LN_EPS) * lvg_ref[:, GD * g:GD * (g + 1)] + lvb_ref[:, GD * g:GD * (g + 1)]
        sp = jnp.concatenate(
            [_mm3(ws_ref[g], vn[GD * c:GD * (c + 1), :]) + bs_ref[g] for c in range(tm // GD)], axis=0)
        parts.append((u * sp).astype(BF16))
    y = jnp.concatenate([ya_ref[...]] + parts, axis=1)
    o_ref[...] = x_ref[...] + _mod_part(mod_ref[0], 2) * _dot(y, wo_ref[...])


def _post(x, ya, ub, mod, lvg, lvb, ws, bsb, wo, rowmap, tm):
    n = x.shape[0]
    c2 = lambda i: (0, 0)
    c3 = lambda i: (0, 0, 0)
    return pl.pallas_call(
        _post_kernel,
        out_shape=jax.ShapeDtypeStruct((n, D), F32),
        grid=(n // tm,),
        in_specs=[pl.BlockSpec((tm, D), lambda i: (i, 0)),
                  pl.BlockSpec((tm, A_W), lambda i: (i, 0)),
                  pl.BlockSpec((tm, 2 * B_W), lambda i: (i, 0)),
                  pl.BlockSpec((1, 1, 6 * D), rowmap),
                  pl.BlockSpec((1, B_W), c2), pl.BlockSpec((1, B_W), c2),
                  pl.BlockSpec(ws.shape, c3), pl.BlockSpec(bsb.shape, c3),
                  pl.BlockSpec((D, D), c2)],
        out_specs=pl.BlockSpec((tm, D), lambda i: (i, 0)),
        compiler_params=_cparams(("arbitrary",)),
        name="gmlp_outproj",
    )(x, ya, ub, mod, lvg, lvb, ws, bsb, wo)


def _query_kernel(x_ref, g_ref, mod_ref, wq_ref, hq_ref, q_ref):
    h = _norm_mod(x_ref[...], g_ref[...], mod_ref[0], 3, 4).astype(BF16)
    hq_ref[...] = h
    q_ref[...] = _dot(h, wq_ref[...])


def _query(x, g, mod, wq, rowmap, tm):
    n = x.shape[0]
    nq = wq.shape[1]
    return pl.pallas_call(
        _query_kernel,
        out_shape=(jax.ShapeDtypeStruct((n, D), BF16), jax.ShapeDtypeStruct((n, nq), F32)),
        grid=(n // tm,),
        in_specs=[pl.BlockSpec((tm, D), lambda i: (i, 0)),
                  pl.BlockSpec((1, D), lambda i: (0, 0)),
                  pl.BlockSpec((1, 1, 6 * D), rowmap),
                  pl.BlockSpec((D, nq), lambda i: (0, 0))],
        out_specs=(pl.BlockSpec((tm, D), lambda i: (i, 0)), pl.BlockSpec((tm, nq), lambda i: (i, 0))),
        compiler_params=_cparams(("arbitrary",)),
        name="peer_query",
    )(x, g, mod, wq)


def _extract_top(s, rowid, n_rows, count):
    rank = jnp.full(s.shape, float(count), F32)
    vals = []
    for r in range(count):
        m = jnp.max(s, axis=0, keepdims=True)
        idx = jnp.min(jnp.where(s == m, rowid, float(n_rows)), axis=0, keepdims=True)
        hit = rowid == idx
        rank = jnp.where(hit, float(r), rank)
        s = jnp.where(hit, -jnp.inf, s)
        vals.append(m)
    return rank, vals


_CAND_Q = [TOPK // (p + 1) for p in range(TOPK)]


def _pull_distinct(s, count):
    vals = []
    for _ in range(count):
        m = jnp.max(s, axis=0, keepdims=True)
        s = jnp.where(s == m, -jnp.inf, s)
        vals.append(m)
    return vals


def _candidates(a, b, tb):
    bmat = jnp.concatenate(b, axis=0)
    amat = jnp.concatenate(a, axis=0)
    row8 = lax.broadcasted_iota(jnp.int32, (8, tb), 0)
    blocks = [a[0] + bmat]
    for p in range(1, 8):
        blocks.append(jnp.where(row8 < _CAND_Q[p], a[p] + bmat[:8, :], -jnp.inf))
    blocks.append(amat[8:, :] + b[0])
    return jnp.concatenate(blocks, axis=0)


def _cand_counts(self32):
    cnt = [jnp.sum(self32[0:16], axis=0, keepdims=True)]
    for p in range(1, 8):
        cnt.append(jnp.sum(self32[8 + 8 * p:16 + 8 * p], axis=0, keepdims=True))
    for p in range(8, 16):
        cnt.append(self32[72 + (p - 8):73 + (p - 8)])
    return cnt


def _topk_kernel(q_ref, sk_ref, cnt1_ref, g1_ref, rank2_ref, g2_ref):
    tb = q_ref.shape[0]
    q = q_ref[...]
    s1 = _mm3(sk_ref[0], q[:, :N_KEYS], _dot_tb)
    s2 = _mm3(sk_ref[1], q[:, N_KEYS:], _dot_tb)

    a = _pull_distinct(s1, TOPK)
    b = _pull_distinct(s2, TOPK)
    cand = _candidates(a, b, tb)
    cthr = _pull_distinct(cand, TOPK)[TOPK - 1]
    sel = cand >= cthr
    top = a[0] + b[0]
    z = jnp.sum(jnp.where(sel, jnp.exp(cand - top), 0.0), axis=0, keepdims=True)
    self32 = sel.astype(F32)
    cnt = _cand_counts(self32)
    cnt1 = jnp.zeros((N_KEYS, tb), F32)
    rank2 = jnp.zeros((N_KEYS, tb), F32)
    for p in range(TOPK):
        cnt1 = jnp.where(s1 == a[p], cnt[p], cnt1)
        rank2 = rank2 + (s2 < b[p]).astype(F32)
    cnt1_ref[0] = cnt1
    g1_ref[0] = jnp.exp(s1 - a[0]) / z
    rank2_ref[0] = rank2.astype(rank2_ref.dtype)
    g2_ref[0] = jnp.exp(s2 - b[0]).astype(g2_ref.dtype)

    n1 = jnp.sum((s1 >= a[TOPK - 1]).astype(F32), axis=0, keepdims=True)
    n2 = jnp.sum((s2 >= b[TOPK - 1]).astype(F32), axis=0, keepdims=True)
    nc = jnp.sum(self32, axis=0, keepdims=True)
    bad = jnp.abs(n1 - TOPK) + jnp.abs(n2 - TOPK) + jnp.abs(nc - TOPK)

    @pl.when(jnp.max(bad) > 0.0)
    def _():
        rowid = lax.broadcasted_iota(jnp.int32, (N_KEYS, tb), 0).astype(F32)
        rank1, ax = _extract_top(s1, rowid, N_KEYS, TOPK)
        rank2x, bx = _extract_top(s2, rowid, N_KEYS, TOPK)
        candx = _candidates(ax, bx, tb)
        n_cand = candx.shape[0]
        crow = lax.broadcasted_iota(jnp.int32, (n_cand, tb), 0).astype(F32)
        crank, _ = _extract_top(candx, crow, n_cand, TOPK)
        selx = crank < float(TOPK)
        zx = jnp.sum(jnp.where(selx, jnp.exp(candx - top), 0.0), axis=0, keepdims=True)
        cntx = _cand_counts(selx.astype(F32))
        cnt1x = jnp.zeros((N_KEYS, tb), F32)
        for p in range(TOPK):
            cnt1x = jnp.where(rank1 == float(p), cntx[p], cnt1x)
        cnt1_ref[0] = cnt1x
        g1_ref[0] = jnp.exp(s1 - a[0]) / zx
        rank2_ref[0] = rank2x.astype(rank2_ref.dtype)


def _topk(q, sk, tb):
    n = q.shape[0]
    shp = jax.ShapeDtypeStruct((PEER_HEADS, N_KEYS, n), F32)
    shp16 = jax.ShapeDtypeStruct((PEER_HEADS, N_KEYS, n), BF16)
    ospec = pl.BlockSpec((1, N_KEYS, tb), lambda i, h: (h, 0, i))
    return pl.pallas_call(
        _topk_kernel,
        out_shape=(shp, shp, shp16, shp16),
        grid=(n // tb, PEER_HEADS),
        in_specs=[pl.BlockSpec((tb, 2 * N_KEYS), lambda i, h: (i, h)),
                  pl.BlockSpec((2, N_KEYS, N_KEYS), lambda i, h: (0, 0, 0))],
        out_specs=(ospec, ospec, ospec, ospec),
        compiler_params=_cparams(("arbitrary", "arbitrary")),
        name="peer_topk",
    )(q, sk)


I1_TILE = 8


GATE_LANES = 512
GATE_ROWS = 16
ACT_ROWS = 256


def _experts_kernel(hq_ref, eu_ref, evt_ref, cnt1_ref, g1_ref, rank2_ref, g2_ref,
                    o_ref, acc_ref, w_ref, cb_ref, gb_ref):
    j = pl.program_id(1)
    tm = hq_ref.shape[0]

    @pl.when(j == 0)
    def _():
        acc_ref[...] = jnp.zeros_like(acc_ref)

    for h in range(PEER_HEADS):
        for ii in range(I1_TILE):
            cb_ref[h, ii] = jnp.broadcast_to(cnt1_ref[h, ii:ii + 1, :], (GATE_ROWS, tm)).astype(BF16)
            gb_ref[h, ii] = jnp.broadcast_to(g1_ref[h, ii:ii + 1, :], (GATE_ROWS, tm)).astype(BF16)

    n_rg = N_KEYS // GATE_ROWS
    for lc in range(tm // GATE_LANES):
        lanes = slice(GATE_LANES * lc, GATE_LANES * (lc + 1))
        for ii0 in range(0, I1_TILE, 2):
            for rg0 in range(0, n_rg, 2):
                accs = [[jnp.zeros((GATE_ROWS, GATE_LANES), BF16) for _ in range(2)] for _ in range(2)]
                for h in range(PEER_HEADS):
                    rk = [rank2_ref[h, GATE_ROWS * (rg0 + b):GATE_ROWS * (rg0 + b + 1), lanes] for b in range(2)]
                    g2 = [g2_ref[h, GATE_ROWS * (rg0 + b):GATE_ROWS * (rg0 + b + 1), lanes] for b in range(2)]
                    for a in range(2):
                        c = cb_ref[h, ii0 + a, :, lanes]
                        g = gb_ref[h, ii0 + a, :, lanes]
                        for b in range(2):
                            accs[a][b] = accs[a][b] + jnp.where(rk[b] < c, g2[b] * g, jnp.zeros_like(g))
                for a in range(2):
                    for b in range(2):
                        r0 = N_KEYS * (ii0 + a) + GATE_ROWS * (rg0 + b)
                        w_ref[r0:r0 + GATE_ROWS, lanes] = accs[a][b]

    hq = hq_ref[...]
    for rc in range(w_ref.shape[0] // ACT_ROWS):
        rows = slice(ACT_ROWS * rc, ACT_ROWS * (rc + 1))
        act = _dot_tb(eu_ref[rows, :], hq)
        gel = act * (lax.erf(act * (1.0 / math.sqrt(2.0))) + 1.0) * 0.5
        w_ref[rows, :] = w_ref[rows, :] * gel.astype(BF16)
    acc_ref[...] += _dot(evt_ref[...], w_ref[...])

    @pl.when(j == pl.num_programs(1) - 1)
    def _():
        o_ref[...] = acc_ref[...].T


def _experts(hq, eu, evt, cnt1, g1, rank2, g2, tm):
    n = hq.shape[0]
    n_exp = eu.shape[0]
    te = I1_TILE * N_KEYS
    assert tm % GATE_LANES == 0 and te % ACT_ROWS == 0 and n % tm == 0
    return pl.pallas_call(
        _experts_kernel,
        out_shape=jax.ShapeDtypeStruct((n, D), F32),
        grid=(n // tm, n_exp // te),
        in_specs=[pl.BlockSpec((tm, D), lambda i, j: (i, 0)),
                  pl.BlockSpec((te, D), lambda i, j: (j, 0)),
                  pl.BlockSpec((D, te), lambda i, j: (0, j)),
                  pl.BlockSpec((PEER_HEADS, I1_TILE, tm), lambda i, j: (0, j, i)),
                  pl.BlockSpec((PEER_HEADS, I1_TILE, tm), lambda i, j: (0, j, i)),
                  pl.BlockSpec((PEER_HEADS, N_KEYS, tm), lambda i, j: (0, 0, i)),
                  pl.BlockSpec((PEER_HEADS, N_KEYS, tm), lambda i, j: (0, 0, i))],
        out_specs=pl.BlockSpec((tm, D), lambda i, j: (i, 0)),
        scratch_shapes=[pltpu.VMEM((D, tm), F32), pltpu.VMEM((te, tm), BF16),
                        pltpu.VMEM((PEER_HEADS, I1_TILE, GATE_ROWS, tm), BF16),
                        pltpu.VMEM((PEER_HEADS, I1_TILE, GATE_ROWS, tm), BF16)],
        compiler_params=_cparams(("arbitrary", "arbitrary")),
        name="peer_experts",
    )(hq, eu, evt, cnt1, g1, rank2, g2)


def _pw1_kernel(x_ref, pe_ref, modp_ref, g_ref, mod_ref, wa_ref, wb_ref, ba_ref, bb_ref, o_ref, x1_ref):
    x1 = x_ref[...] + _mod_part(modp_ref[0], 5) * pe_ref[...]
    x1_ref[...] = x1
    h = _norm_mod(x1, g_ref[...], mod_ref[0], 0, 1).astype(BF16)
    a = _dot(h, wa_ref[...]) + ba_ref[...]
    b = _dot(h, wb_ref[...]) + bb_ref[...]
    o_ref[...] = a * jax.nn.sigmoid(b)


def _pw1(x, pe, mod_prev, g, mod, wa, wb, ba, bb, rowmap, tm):
    n = x.shape[0]
    c2 = lambda i: (0, 0)
    tok = pl.BlockSpec((tm, D), lambda i: (i, 0))
    return pl.pallas_call(
        _pw1_kernel,
        out_shape=(jax.ShapeDtypeStruct((n, D), F32), jax.ShapeDtypeStruct((n, D), F32)),
        grid=(n // tm,),
        in_specs=[tok, tok,
                  pl.BlockSpec((1, 1, 6 * D), rowmap),
                  pl.BlockSpec((1, D), c2),
                  pl.BlockSpec((1, 1, 6 * D), rowmap),
                  pl.BlockSpec((D, D), c2), pl.BlockSpec((D, D), c2),
                  pl.BlockSpec((1, D), c2), pl.BlockSpec((1, D), c2)],
        out_specs=(tok, tok),
        compiler_params=_cparams(("arbitrary",)),
        name="conv_pw1_glu",
    )(x, pe, mod_prev, g, mod, wa, wb, ba, bb)


def _conv_kernel(n_ctx_blocks, seg_ctx, seg_lat,
                 z_ref, x_ref, mod_ref, ck_ref, cb_ref, lg_ref, lb_ref, w2_ref, b2_ref, o_ref, pad_ref):
    tm = z_ref.shape[0]
    i = pl.program_id(0)
    seg = jnp.where(i < n_ctx_blocks, seg_ctx, seg_lat)
    zeros = jnp.zeros((CONV_PAD, D), F32)
    pad_ref[0:CONV_PAD, :] = zeros
    pad_ref[CONV_PAD + tm:, :] = zeros
    pad_ref[CONV_PAD:CONV_PAD + tm, :] = z_ref[...]
    pos = jnp.bitwise_and(lax.broadcasted_iota(jnp.int32, (tm, D), 0), seg - 1)
    acc = jnp.zeros((tm, D), F32)
    for k in range(CONV_W):
        off = k - CONV_HALF
        xs = pad_ref[CONV_PAD + off:CONV_PAD + off + tm, :]
        if off < 0:
            xs = jnp.where(pos >= -off, xs, 0.0)
        elif off > 0:
            xs = jnp.where(pos < seg - off, xs, 0.0)
        acc = acc + xs * ck_ref[k:k + 1, :]
    z = acc + cb_ref[...]
    m = jnp.mean(z, axis=-1, keepdims=True)
    dv = z - m
    var = jnp.mean(dv * dv, axis=-1, keepdims=True)
    zn = dv * lax.rsqrt(var + LN_EPS) * lg_ref[...] + lb_ref[...]
    act = (zn * jax.nn.sigmoid(zn)).astype(BF16)
    y = _dot(act, w2_ref[...]) + b2_ref[...]
    o_ref[...] = x_ref[...] + _mod_part(mod_ref[0], 2) * y


def _conv(z, x, mod, ck, cb, lg, lb, w2, b2, rowmap, tm, n_ctx_blocks, seg_ctx, seg_lat):
    n = x.shape[0]
    c2 = lambda i: (0, 0)
    return pl.pallas_call(
        functools.partial(_conv_kernel, n_ctx_blocks, seg_ctx, seg_lat),
        out_shape=jax.ShapeDtypeStruct((n, D), F32),
        grid=(n // tm,),
        in_specs=[pl.BlockSpec((tm, D), lambda i: (i, 0)),
                  pl.BlockSpec((tm, D), lambda i: (i, 0)),
                  pl.BlockSpec((1, 1, 6 * D), rowmap),
                  pl.BlockSpec(ck.shape, c2),
                  pl.BlockSpec((1, D), c2), pl.BlockSpec((1, D), c2), pl.BlockSpec((1, D), c2),
                  pl.BlockSpec((D, D), c2), pl.BlockSpec((1, D), c2)],
        out_specs=pl.BlockSpec((tm, D), lambda i: (i, 0)),
        scratch_shapes=[pltpu.VMEM((tm + 2 * CONV_PAD, D), F32)],
        compiler_params=_cparams(("arbitrary",)),
        name="conv_dw_ln_pw2",
    )(z, x, mod, ck, cb, lg, lb, w2, b2)


def _final_kernel(x_ref, pe_ref, mod_ref, g_ref, o_ref):
    x = x_ref[...] + _mod_part(mod_ref[0], 5) * pe_ref[...]
    ms = jnp.mean(x * x, axis=-1, keepdims=True)
    o_ref[...] = x * lax.rsqrt(ms + EPS) * g_ref[...]


def _final_norm(x, pe, mod, g, rowmap, tm):
    n = x.shape[0]
    tok = pl.BlockSpec((tm, D), lambda i: (i, 0))
    return pl.pallas_call(
        _final_kernel,
        out_shape=jax.ShapeDtypeStruct((n, D), F32),
        grid=(n // tm,),
        in_specs=[tok, tok, pl.BlockSpec((1, 1, 6 * D), rowmap), pl.BlockSpec((1, D), lambda i: (0, 0))],
        out_specs=tok,
        compiler_params=_cparams(("arbitrary",)),
        name="final_norm",
    )(x, pe, mod, g)


def _peer_layer(x, l, mod_l, norm2, w_query, sub_keys, expert_u, expert_v, rowmap_of):
    hq, q = _query(x, norm2[l][None, :], mod_l, w_query[l].astype(BF16), rowmap_of(512), 512)
    cnt1, g1, rank2, g2 = _topk(q, sub_keys[l], 512)
    eu = expert_u[l].astype(BF16)
    evt = expert_v[l].astype(BF16).T
    return _experts(hq, eu, evt, cnt1, g1, rank2, g2, 1024)


def kernel(x_prompt, x_sample, state_rwkv, c, c_ctx, w_mod, b_mod, norm1, norm2, norm_f, w_in_ab, mu_a, w0, w2, a0, a2, g2, k_k, k_a, r_k, lnx_g, lnx_b, lnv_g, lnv_b, w_s, b_s, w_out_ab, w_pw1, b_pw1, conv_k, conv_b, lnc_g, lnc_b, w_pw2, b_pw2, w_query, sub_keys, expert_u, expert_v):
    n_ctx, ctx_len, _ = x_prompt.shape
    n_lat, lat_len, _ = x_sample.shape
    nc_tok = n_ctx * ctx_len
    grid_w = 64
    x = jnp.concatenate([x_prompt.reshape(nc_tok, D), x_sample.reshape(n_lat * lat_len, D)], axis=0)

    cvec8 = jnp.zeros((8, D), F32).at[0].set(c_ctx).at[1:1 + n_lat].set(c)
    mod = _modulation(cvec8, w_mod, b_mod)
    mods = [mod[l].reshape(8, 1, 6 * D) for l in range(mod.shape[0])]
    rowmap_of = lambda tm: _mod_row_map(nc_tok, lat_len, tm)

    w_in = w_in_ab[0]
    za, ub = _inproj(x, norm1[0][None, :], mods[0], w_in[:, :A_COLS].astype(BF16),
                     w_in[:, A_COLS:].astype(BF16), rowmap_of(512), 512)
    row = lambda a: a.reshape(1, -1)
    pad_lora = lambda w, off: jnp.zeros((2, 128, A_W), F32).at[0, :HD].set(w[0]).at[1, HD:].set(w[1])
    heads = jnp.arange(A_W) // HD
    hsum = (heads[:, None] == heads[None, :]).astype(BF16)
    rparams = (row(mu_a[0]), w0[0], pad_lora(w2[0], 0), a0[0], pad_lora(a2[0], 0), g2[0],
               row(k_k[0]), row(k_a[0]), row(r_k[0]), row(lnx_g[0]), row(lnx_b[0]), hsum)
    zero_state = jnp.zeros((n_ctx, 2 * NPAIR, 128, 128), F32)
    ya_c, sfin = _rwkv(za, zero_state, ctx_len, 0, rparams)
    ya_s, _ = _rwkv(za, _states_to_blockdiag(state_rwkv[:, 0]), lat_len, nc_tok, rparams)
    ya = jnp.concatenate([ya_c, ya_s], axis=0)
    bsb = jnp.broadcast_to(b_s[0][:, :, None], (b_s.shape[1], GD, GD))
    x = _post(x, ya, ub, mods[0], row(lnv_g[0]), row(lnv_b[0]), w_s[0], bsb,
              w_out_ab[0].astype(BF16), rowmap_of(512), 512)
    pe = _peer_layer(x, 0, mods[0], norm2, w_query, sub_keys, expert_u, expert_v, rowmap_of)

    glu, x = _pw1(x, pe, mods[0], norm1[1][None, :], mods[1], w_pw1[0][:, :D].astype(BF16),
                  w_pw1[0][:, D:].astype(BF16), row(b_pw1[0][:D]), row(b_pw1[0][D:]), rowmap_of(512), 512)
    ck = jnp.zeros((32, D), F32).at[:CONV_W].set(conv_k[0])
    x = _conv(glu, x, mods[1], ck, row(conv_b[0]), row(lnc_g[0]), row(lnc_b[0]),
              w_pw2[0].astype(BF16), row(b_pw2[0]), rowmap_of(256), 256,
              nc_tok // 256, ctx_len, grid_w)
    pe = _peer_layer(x, 1, mods[1], norm2, w_query, sub_keys, expert_u, expert_v, rowmap_of)

    y = _final_norm(x, pe, mods[1], norm_f[None, :], rowmap_of(512), 512)
    y_prompt = y[:nc_tok].reshape(n_ctx, ctx_len, D)
    y_sample = y[nc_tok:].reshape(n_lat, lat_len, D)
    new_state = _blockdiag_to_states(sfin)[:, None].astype(x_prompt.dtype)
    return (y_prompt, y_sample, new_state)
```

```python
import functools
import math

import jax
import jax.numpy as jnp
from jax import lax
from jax.experimental import pallas as pl
from jax.experimental.pallas import tpu as pltpu

F32 = jnp.float32
BF16 = jnp.bfloat16

D = 1024
HD = 64
CH = 64
A_W = 512
NPAIR = A_W // 128
R_OFF, K_OFF, V_OFF = 0, 512, 1024
WLO_OFF, ALO_OFF, GLO_OFF, A_COLS = 1536, 1664, 1792, 1920
B_W = 512
GD = 128
CONV_W = 31
CONV_HALF = 15
CONV_PAD = 16
N_KEYS = 128
TOPK = 16
PEER_HEADS = 8
EPS = 1e-6
LN_EPS = 1e-5
GN_EPS = 64e-5
VMEM_LIMIT = 56 * 1024 * 1024


def _cparams(sem):
    return pltpu.CompilerParams(dimension_semantics=sem, vmem_limit_bytes=VMEM_LIMIT)


def _dot(a, b):
    return jnp.dot(a, b, preferred_element_type=F32)


def _dot_tb(a, b):
    return lax.dot_general(a, b, (((1,), (1,)), ((), ())), preferred_element_type=F32)


def _dot_ta(a, b):
    return lax.dot_general(a, b, (((0,), (0,)), ((), ())), preferred_element_type=F32)


def _split2(x):
    hi = x.astype(BF16)
    lo = (x - hi.astype(F32)).astype(BF16)
    return hi, lo


def _split3(x):
    hi = x.astype(BF16)
    r1 = x - hi.astype(F32)
    mid = r1.astype(BF16)
    lo = (r1 - mid.astype(F32)).astype(BF16)
    return hi, mid, lo


def _mm3(a, b, dot=_dot):
    ah, al = _split2(a)
    bh, bl = _split2(b)
    return dot(ah, bh) + dot(ah, bl) + dot(al, bh)


def _stack4(p, axis, lhs):
    hi, lo = p
    return jnp.concatenate([hi, hi, lo, lo] if lhs else [hi, lo, hi, lo], axis=axis)


def _mm4(a_p, b_p):
    return _dot(_stack4(a_p, 1, True), _stack4(b_p, 0, False))


def _mm4_tb(a_p, b_p):
    return _dot_tb(_stack4(a_p, 1, True), _stack4(b_p, 1, False))


def _mm4_ta(a_p, b_p):
    return _dot_ta(_stack4(a_p, 0, True), _stack4(b_p, 0, False))


def _mm_exact_lhs(m_bf16, x):
    hi, mid, lo = _split3(x)
    return _dot(m_bf16, hi) + _dot(m_bf16, mid) + _dot(m_bf16, lo)


def _mm_exact_rhs(x, m_bf16):
    hi, mid, lo = _split3(x)
    return _dot(hi, m_bf16) + _dot(mid, m_bf16) + _dot(lo, m_bf16)


def _mod_part(mod_row, j):
    return mod_row[:, j * D:(j + 1) * D]


def _norm_mod(x, g, mod_row, j_shift, j_scale):
    ms = jnp.mean(x * x, axis=-1, keepdims=True)
    y = x * lax.rsqrt(ms + EPS) * g
    return y * (1.0 + _mod_part(mod_row, j_scale)) + _mod_part(mod_row, j_shift)


def _mod_row_map(n_ctx_tokens, seq_tokens, tm):
    ncb = n_ctx_tokens // tm
    per = seq_tokens // tm

    def imap(i, *_):
        return (jnp.where(i < ncb, 0, 1 + (i - ncb) // per), 0, 0)
    return imap


def _mod_kernel(c_ref, w_ref, b_ref, o_ref):
    c = c_ref[...]
    s = c * jax.nn.sigmoid(c)
    o_ref[0] = _mm3(s, w_ref[0]) + b_ref[0]


def _modulation(cvec8, w_mod, b_mod):
    depth, _, n6 = w_mod.shape
    tn = 1536
    return pl.pallas_call(
        _mod_kernel,
        out_shape=jax.ShapeDtypeStruct((depth, 8, n6), F32),
        grid=(depth, n6 // tn),
        in_specs=[pl.BlockSpec((8, D), lambda l, j: (0, 0)),
                  pl.BlockSpec((1, D, tn), lambda l, j: (l, 0, j)),
                  pl.BlockSpec((1, 1, tn), lambda l, j: (l, 0, j))],
        out_specs=pl.BlockSpec((1, 8, tn), lambda l, j: (l, 0, j)),
        compiler_params=_cparams(("arbitrary", "arbitrary")),
        name="modulation",
    )(cvec8, w_mod, b_mod.reshape(depth, 1, n6))


def _inproj_kernel(x_ref, g_ref, mod_ref, wa_ref, wb_ref, za_ref, ub_ref):
    h = _norm_mod(x_ref[...], g_ref[...], mod_ref[0], 0, 1).astype(BF16)
    za_ref[...] = _dot(h, wa_ref[...])
    ub_ref[...] = _dot(h, wb_ref[...])


def _inproj(x, g, mod, wa, wb, rowmap, tm):
    n = x.shape[0]
    return pl.pallas_call(
        _inproj_kernel,
        out_shape=(jax.ShapeDtypeStruct((n, A_COLS), F32), jax.ShapeDtypeStruct((n, 2 * B_W), F32)),
        grid=(n // tm,),
        in_specs=[pl.BlockSpec((tm, D), lambda i: (i, 0)),
                  pl.BlockSpec((1, D), lambda i: (0, 0)),
                  pl.BlockSpec((1, 1, 6 * D), rowmap),
                  pl.BlockSpec((D, A_COLS), lambda i: (0, 0)),
                  pl.BlockSpec((D, 2 * B_W), lambda i: (0, 0))],
        out_specs=(pl.BlockSpec((tm, A_COLS), lambda i: (i, 0)),
                   pl.BlockSpec((tm, 2 * B_W), lambda i: (i, 0))),
        compiler_params=_cparams(("arbitrary",)),
        name="inproj",
    )(x, g, mod, wa, wb)


def _softplus(x):
    return jnp.maximum(x, 0.0) + jnp.log1p(jnp.exp(-jnp.abs(x)))


def _rwkv_kernel(za_ref, s0_ref, mu_ref, w0_ref, w2_ref, a0_ref, a2_ref, g2_ref, kk_ref, ka_ref,
                 rk_ref, lng_ref, lnb_ref, hsum_ref, ya_ref, sfin_ref,
                 prep_ref, gam_ref, st_ref, y_ref, bon_ref, gate_ref):
    t_len = za_ref.shape[0]
    nc = t_len // CH
    mu = mu_ref[...]
    hsum = hsum_ref[...]
    row_c = lax.broadcasted_iota(jnp.int32, (CH, A_COLS), 0)
    ri = lax.broadcasted_iota(jnp.int32, (CH, CH), 0)
    ci = lax.broadcasted_iota(jnp.int32, (CH, CH), 1)
    tri_f = (ci <= ri).astype(BF16)
    tri_b = (ci >= ri).astype(BF16)
    r2 = lax.broadcasted_iota(jnp.int32, (2 * CH, 2 * CH), 0)
    c2 = lax.broadcasted_iota(jnp.int32, (2 * CH, 2 * CH), 1)
    r2m = jnp.bitwise_and(r2, CH - 1)
    c2m = jnp.bitwise_and(c2, CH - 1)
    eye2 = (r2 == c2).astype(F32)
    lane_lo = lax.broadcasted_iota(jnp.int32, (CH, 2 * HD), 1) < HD

    st_ref[...] = s0_ref[0]

    def stack(x):
        return jnp.concatenate([jnp.where(lane_lo, x, 0.0), jnp.where(lane_lo, 0.0, x)], axis=0)

    def mixed_chunk(c):
        start = pl.multiple_of(c * CH, CH)
        zc = za_ref[pl.ds(start, CH), :]
        p0 = pl.multiple_of(jnp.maximum(start - 8, 0), 8)
        n0 = pl.multiple_of(jnp.minimum(start + CH, t_len - 8), 8)
        prow = za_ref[pl.ds(p0, 8), :][7:8, :] * (start > 0).astype(F32)
        nrow = za_ref[pl.ds(n0, 8), :][0:1, :] * (start + CH < t_len).astype(F32)
        prev = jnp.where(row_c == 0, prow, pltpu.roll(zc, 1, 0))
        nxt = jnp.where(row_c == CH - 1, nrow, pltpu.roll(zc, CH - 1, 0))
        return zc + mu * (0.5 * (prev + nxt) - zc), start

    def prep(c, d):
        zm, start = mixed_chunk(c)
        r = zm[:, R_OFF:R_OFF + A_W]
        k = zm[:, K_OFF:K_OFF + A_W]
        v = zm[:, V_OFF:V_OFF + A_W]
        wlo2 = zm[:, WLO_OFF:WLO_OFF + 128]
        alo2 = zm[:, ALO_OFF:ALO_OFF + 128]
        kkr = k * kk_ref[...]
        n2 = _mm_exact_rhs(kkr * kkr, hsum)
        kk = kkr / jnp.maximum(jnp.sqrt(n2), 1e-12)
        wl = w0_ref[d:d + 1, :] + _mm3(jnp.tanh(wlo2), w2_ref[d])
        ld = -jnp.exp(-_softplus(-wl) - 0.5)
        ag = jax.nn.sigmoid(a0_ref[d:d + 1, :] + _mm3(alo2, a2_ref[d]))
        kd = k * (1.0 + (ag - 1.0) * ka_ref[...])
        bv = kk * ag
        cum = _mm_exact_lhs(tri_f if d == 0 else tri_b, ld)
        gam = jnp.exp(cum)
        gam_ex = jnp.exp(cum - ld)
        igam = jnp.exp(-cum)
        ops = (-kk * gam_ex, r * gam, bv * igam, kd * igam, v)
        for j, o in enumerate(ops):
            for p in range(NPAIR):
                prep_ref[d, j, p] = o[:, 128 * p:128 * (p + 1)]
        gtot = gam[CH - 1:CH, :] if d == 0 else gam[0:1, :]
        for p in range(NPAIR):
            gam_ref[d, p] = jnp.broadcast_to(gtot[:, 128 * p:128 * (p + 1)], (8, 128))
        bon_ref[d, pl.ds(start, CH), :] = _mm_exact_rhs(r * kd * rk_ref[...], hsum) * v
        if d == 0:
            glo = zm[:, GLO_OFF:GLO_OFF + 128]
            gate_ref[pl.ds(start, CH), :] = _mm3(jax.nn.sigmoid(glo), g2_ref[...])

    def solve_group(d, i):
        fwd = d == 0
        c = i if fwd else nc - 1 - i
        start = pl.multiple_of(c * CH, CH)
        n2 = 2 * CH
        ps = range(NPAIR)
        order = (r2m - c2m) if fwd else (c2m - r2m)
        strict = order > 0
        incl = order >= 0
        a_s = [stack(prep_ref[d, 0, p]) for p in ps]
        r_s = [stack(prep_ref[d, 1, p]) for p in ps]
        b_p = [_split2(stack(prep_ref[d, 2, p])) for p in ps]
        k_p = [_split2(stack(prep_ref[d, 3, p])) for p in ps]
        v_p = [_split2(stack(prep_ref[d, 4, p])) for p in ps]
        sc = [_mm4_tb(_split2(jnp.concatenate([a_s[p], r_s[p]], axis=0)),
                      tuple(jnp.concatenate([x, y], axis=0) for x, y in zip(b_p[p], k_p[p]))) for p in ps]
        sab = [jnp.where(strict, sc[p][:n2, :n2], 0.0) for p in ps]
        sak = [jnp.where(strict, sc[p][:n2, n2:], 0.0) for p in ps]
        nrb = [jnp.where(incl, sc[p][n2:, :n2], 0.0) for p in ps]
        nrk = [jnp.where(incl, sc[p][n2:, n2:], 0.0) for p in ps]
        pw = []
        for p in ps:
            pw_p = _split2(sab[p])
            pw.append(_mm4(pw_p, pw_p))
        tinv = [eye2 + sab[p] for p in ps]
        n_dbl = int(math.log2(CH))
        for kk in range(1, n_dbl):
            for p in ps:
                pw_p = _split2(pw[p])
                if kk < n_dbl - 1:
                    both = _mm4(pw_p, _split2(jnp.concatenate([pw[p], tinv[p]], axis=1)))
                    pw[p] = both[:, :n2]
                    tinv[p] = tinv[p] + both[:, n2:]
                else:
                    tinv[p] = tinv[p] + _mm4(pw_p, _split2(tinv[p]))
        sakv = [_mm4(_split2(sak[p]), v_p[p]) for p in ps]
        au_p = [_split2(_mm4(_split2(tinv[p]), _split2(jnp.concatenate([a_s[p], sakv[p]], axis=1))))
                for p in ps]
        ry = [_mm4(_split2(nrb[p]), au_p[p]) for p in ps]
        nv = [_mm4(_split2(nrk[p]), v_p[p]) for p in ps]
        gh = [_mm4_ta(au_p[p], b_p[p]) for p in ps]
        vk = [_mm4_ta(v_p[p], k_p[p]) for p in ps]
        s_p = [_split2(st_ref[d * NPAIR + p]) for p in ps]
        ys = [_mm4_tb(_split2(r_s[p] + ry[p][:, :n2]), s_p[p]) for p in ps]
        for p in ps:
            gt = gam_ref[d, p][0:1, :]
            g_t = (eye2 + gh[p][:n2, :]) * gt
            h_t = (gh[p][n2:, :] + vk[p]) * gt
            y_all = ys[p] + ry[p][:, n2:] + nv[p]
            y_ref[d, p, pl.ds(start, CH), :] = y_all[:CH, :] + y_all[CH:, :]
            st_ref[d * NPAIR + p] = _mm4(s_p[p], _split2(g_t)) + h_t

    def chunk_body(i, carry):
        prep(i, 0)
        prep(nc - 1 - i, 1)
        solve_group(0, i)
        solve_group(1, i)
        return carry

    lax.fori_loop(0, nc, chunk_body, 0)

    def out_body(c, carry):
        start = pl.multiple_of(c * CH, CH)
        rows = pl.ds(start, CH)
        ysum = jnp.concatenate([y_ref[0, p, rows, :] + y_ref[1, p, rows, :] for p in range(NPAIR)], axis=1)
        mean = _mm_exact_rhs(ysum, hsum) * (1.0 / HD)
        dev = ysum - mean
        var = _mm_exact_rhs(dev * dev, hsum) * (1.0 / HD)
        gn = dev * lax.rsqrt(var + GN_EPS) * lng_ref[...] + lnb_ref[...]
        bonus = bon_ref[0, rows, :] + bon_ref[1, rows, :]
        ya_ref[rows, :] = ((gn + bonus) * gate_ref[rows, :]).astype(ya_ref.dtype)
        return carry

    lax.fori_loop(0, nc, out_body, 0)
    sfin_ref[0] = st_ref[...]


def _rwkv(za, s0bd, t_len, first_token, params):
    n_seq = s0bd.shape[0]
    blk0 = first_token // t_len
    nq = 2 * NPAIR
    full = lambda a: pl.BlockSpec(a.shape, lambda s, _n=a.ndim: (0,) * _n)
    return pl.pallas_call(
        _rwkv_kernel,
        out_shape=(jax.ShapeDtypeStruct((n_seq * t_len, A_W), BF16),
                   jax.ShapeDtypeStruct((n_seq, nq, 128, 128), F32)),
        grid=(n_seq,),
        in_specs=[pl.BlockSpec((t_len, A_COLS), lambda s: (s + blk0, 0)),
                  pl.BlockSpec((1, nq, 128, 128), lambda s: (s, 0, 0, 0))] + [full(a) for a in params],
        out_specs=(pl.BlockSpec((t_len, A_W), lambda s: (s, 0)),
                   pl.BlockSpec((1, nq, 128, 128), lambda s: (s, 0, 0, 0))),
        scratch_shapes=[pltpu.VMEM((2, 5, NPAIR, CH, 128), F32),
                        pltpu.VMEM((2, NPAIR, 8, 128), F32),
                        pltpu.VMEM((nq, 128, 128), F32),
                        pltpu.VMEM((2, NPAIR, t_len, 128), F32),
                        pltpu.VMEM((2, t_len, A_W), F32),
                        pltpu.VMEM((t_len, A_W), F32)],
        compiler_params=_cparams(("arbitrary",)),
        name="rwkv_t%d" % t_len,
    )(za, s0bd, *params)


def _states_to_blockdiag(s):
    n = s.shape[0]
    s = s.reshape(n, 2, NPAIR, 2, HD, HD)
    z = jnp.zeros_like(s[:, :, :, 0])
    top = jnp.concatenate([s[:, :, :, 0], z], axis=-1)
    bot = jnp.concatenate([z, s[:, :, :, 1]], axis=-1)
    return jnp.concatenate([top, bot], axis=-2).reshape(n, 2 * NPAIR, 128, 128)


def _blockdiag_to_states(b):
    n = b.shape[0]
    b = b.reshape(n, 2, NPAIR, 128, 128)
    s = jnp.stack([b[..., :HD, :HD], b[..., HD:, HD:]], axis=3)
    return s.reshape(n, 2, 2 * NPAIR, HD, HD)


def _post_kernel(x_ref, ya_ref, ub_ref, mod_ref, lvg_ref, lvb_ref, ws_ref, bs_ref, wo_ref, o_ref):
    tm = x_ref.shape[0]
    ub = ub_ref[...]
    parts = []
    for g in range(B_W // GD):
        u = ub[:, GD * g:GD * (g + 1)]
        vb = ub[:, B_W + GD * g:B_W + GD * (g + 1)]
        m = jnp.mean(vb, axis=-1, keepdims=True)
        dv = vb - m
        var = jnp.mean(dv * dv, axis=-1, keepdims=True)
        vn = dv * lax.rsqrt(var + LN_EPS) * lvg_ref[:, GD * g:GD * (g + 1)] + lvb_ref[:, GD * g:GD * (g + 1)]
        sp = jnp.concatenate(
            [_mm3(ws_ref[g], vn[GD * c:GD * (c + 1), :]) + bs_ref[g] for c in range(tm // GD)], axis=0)
        parts.append((u * sp).astype(BF16))
    y = jnp.concatenate([ya_ref[...]] + parts, axis=1)
    o_ref[...] = x_ref[...] + _mod_part(mod_ref[0], 2) * _dot(y, wo_ref[...])


def _post(x, ya, ub, mod, lvg, lvb, ws, bsb, wo, rowmap, tm):
    n = x.shape[0]
    c2 = lambda i: (0, 0)
    c3 = lambda i: (0, 0, 0)
    return pl.pallas_call(
        _post_kernel,
        out_shape=jax.ShapeDtypeStruct((n, D), F32),
        grid=(n // tm,),
        in_specs=[pl.BlockSpec((tm, D), lambda i: (i, 0)),
                  pl.BlockSpec((tm, A_W), lambda i: (i, 0)),
                  pl.BlockSpec((tm, 2 * B_W), lambda i: (i, 0)),
                  pl.BlockSpec((1, 1, 6 * D), rowmap),
                  pl.BlockSpec((1, B_W), c2), pl.BlockSpec((1, B_W), c2),
                  pl.BlockSpec(ws.shape, c3), pl.BlockSpec(bsb.shape, c3),
                  pl.BlockSpec((D, D), c2)],
        out_specs=pl.BlockSpec((tm, D), lambda i: (i, 0)),
        compiler_params=_cparams(("arbitrary",)),
        name="gmlp_outproj",
    )(x, ya, ub, mod, lvg, lvb, ws, bsb, wo)


def _query_kernel(x_ref, g_ref, mod_ref, wq_ref, hq_ref, q_ref):
    h = _norm_mod(x_ref[...], g_ref[...], mod_ref[0], 3, 4).astype(BF16)
    hq_ref[...] = h
    q_ref[...] = _dot(h, wq_ref[...])


def _query(x, g, mod, wq, rowmap, tm):
    n = x.shape[0]
    nq = wq.shape[1]
    return pl.pallas_call(
        _query_kernel,
        out_shape=(jax.ShapeDtypeStruct((n, D), BF16), jax.ShapeDtypeStruct((n, nq), F32)),
        grid=(n // tm,),
        in_specs=[pl.BlockSpec((tm, D), lambda i: (i, 0)),
                  pl.BlockSpec((1, D), lambda i: (0, 0)),
                  pl.BlockSpec((1, 1, 6 * D), rowmap),
                  pl.BlockSpec((D, nq), lambda i: (0, 0))],
        out_specs=(pl.BlockSpec((tm, D), lambda i: (i, 0)), pl.BlockSpec((tm, nq), lambda i: (i, 0))),
        compiler_params=_cparams(("arbitrary",)),
        name="peer_query",
    )(x, g, mod, wq)


def _extract_top(s, rowid, n_rows, count):
    rank = jnp.full(s.shape, float(count), F32)
    vals = []
    for r in range(count):
        m = jnp.max(s, axis=0, keepdims=True)
        idx = jnp.min(jnp.where(s == m, rowid, float(n_rows)), axis=0, keepdims=True)
        hit = rowid == idx
        rank = jnp.where(hit, float(r), rank)
        s = jnp.where(hit, -jnp.inf, s)
        vals.append(m)
    return rank, vals


_CAND_Q = [TOPK // (p + 1) for p in range(TOPK)]


def _pull_distinct(s, count):
    vals = []
    for _ in range(count):
        m = jnp.max(s, axis=0, keepdims=True)
        s = jnp.where(s == m, -jnp.inf, s)
        vals.append(m)
    return vals


def _candidates(a, b, tb):
    bmat = jnp.concatenate(b, axis=0)
    amat = jnp.concatenate(a, axis=0)
    row8 = lax.broadcasted_iota(jnp.int32, (8, tb), 0)
    blocks = [a[0] + bmat]
    for p in range(1, 8):
        blocks.append(jnp.where(row8 < _CAND_Q[p], a[p] + bmat[:8, :], -jnp.inf))
    blocks.append(amat[8:, :] + b[0])
    return jnp.concatenate(blocks, axis=0)


def _cand_counts(self32):
    cnt = [jnp.sum(self32[0:16], axis=0, keepdims=True)]
    for p in range(1, 8):
        cnt.append(jnp.sum(self32[8 + 8 * p:16 + 8 * p], axis=0, keepdims=True))
    for p in range(8, 16):
        cnt.append(self32[72 + (p - 8):73 + (p - 8)])
    return cnt


def _topk_kernel(q_ref, sk_ref, cnt1_ref, g1_ref, rank2_ref, g2_ref):
    tb = q_ref.shape[0]
    q = q_ref[...]
    s1 = _mm3(sk_ref[0], q[:, :N_KEYS], _dot_tb)
    s2 = _mm3(sk_ref[1], q[:, N_KEYS:], _dot_tb)

    a = _pull_distinct(s1, TOPK)
    b = _pull_distinct(s2, TOPK)
    cand = _candidates(a, b, tb)
    cthr = _pull_distinct(cand, TOPK)[TOPK - 1]
    sel = cand >= cthr
    top = a[0] + b[0]
    z = jnp.sum(jnp.where(sel, jnp.exp(cand - top), 0.0), axis=0, keepdims=True)
    self32 = sel.astype(F32)
    cnt = _cand_counts(self32)
    cnt1 = jnp.zeros((N_KEYS, tb), F32)
    rank2 = jnp.zeros((N_KEYS, tb), F32)
    for p in range(TOPK):
        cnt1 = jnp.where(s1 == a[p], cnt[p], cnt1)
        rank2 = rank2 + (s2 < b[p]).astype(F32)
    cnt1_ref[0] = cnt1
    g1_ref[0] = jnp.exp(s1 - a[0]) / z
    rank2_ref[0] = rank2.astype(rank2_ref.dtype)
    g2_ref[0] = jnp.exp(s2 - b[0]).astype(g2_ref.dtype)

    n1 = jnp.sum((s1 >= a[TOPK - 1]).astype(F32), axis=0, keepdims=True)
    n2 = jnp.sum((s2 >= b[TOPK - 1]).astype(F32), axis=0, keepdims=True)
    nc = jnp.sum(self32, axis=0, keepdims=True)
    bad = jnp.abs(n1 - TOPK) + jnp.abs(n2 - TOPK) + jnp.abs(nc - TOPK)

    @pl.when(jnp.max(bad) > 0.0)
    def _():
        rowid = lax.broadcasted_iota(jnp.int32, (N_KEYS, tb), 0).astype(F32)
        rank1, ax = _extract_top(s1, rowid, N_KEYS, TOPK)
        rank2x, bx = _extract_top(s2, rowid, N_KEYS, TOPK)
        candx = _candidates(ax, bx, tb)
        n_cand = candx.shape[0]
        crow = lax.broadcasted_iota(jnp.int32, (n_cand, tb), 0).astype(F32)
        crank, _ = _extract_top(candx, crow, n_cand, TOPK)
        selx = crank < float(TOPK)
        zx = jnp.sum(jnp.where(selx, jnp.exp(candx - top), 0.0), axis=0, keepdims=True)
        cntx = _cand_counts(selx.astype(F32))
        cnt1x = jnp.zeros((N_KEYS, tb), F32)
        for p in range(TOPK):
            cnt1x = jnp.where(rank1 == float(p), cntx[p], cnt1x)
        cnt1_ref[0] = cnt1x
        g1_ref[0] = jnp.exp(s1 - a[0]) / zx
        rank2_ref[0] = rank2x.astype(rank2_ref.dtype)


def _topk(q, sk, tb):
    n = q.shape[0]
    shp = jax.ShapeDtypeStruct((PEER_HEADS, N_KEYS, n), F32)
    shp16 = jax.ShapeDtypeStruct((PEER_HEADS, N_KEYS, n), BF16)
    ospec = pl.BlockSpec((1, N_KEYS, tb), lambda i, h: (h, 0, i))
    return pl.pallas_call(
        _topk_kernel,
        out_shape=(shp, shp, shp16, shp16),
        grid=(n // tb, PEER_HEADS),
        in_specs=[pl.BlockSpec((tb, 2 * N_KEYS), lambda i, h: (i, h)),
                  pl.BlockSpec((2, N_KEYS, N_KEYS), lambda i, h: (0, 0, 0))],
        out_specs=(ospec, ospec, ospec, ospec),
        compiler_params=_cparams(("arbitrary", "arbitrary")),
        name="peer_topk",
    )(q, sk)


I1_TILE = 8


GATE_LANES = 256
GATE_BLOCK = (2, 4)
GATE_ROWS = 16
ACT_ROWS = 256


def _experts_kernel(hq_ref, eu_ref, evt_ref, cnt1_ref, g1_ref, rank2_ref, g2_ref,
                    o_ref, acc_ref, w_ref, cb_ref, gb_ref):
    j = pl.program_id(1)
    tm = hq_ref.shape[0]

    @pl.when(j == 0)
    def _():
        acc_ref[...] = jnp.zeros_like(acc_ref)

    for h in range(PEER_HEADS):
        for ii in range(I1_TILE):
            rows = slice(GATE_ROWS * ii, GATE_ROWS * (ii + 1))
            cb_ref[h, rows, :] = jnp.broadcast_to(cnt1_ref[h, ii:ii + 1, :], (GATE_ROWS, tm)).astype(BF16)
            gb_ref[h, rows, :] = jnp.broadcast_to(g1_ref[h, ii:ii + 1, :], (GATE_ROWS, tm)).astype(BF16)

    n_rg = N_KEYS // GATE_ROWS
    na, nb = GATE_BLOCK
    for lc in range(tm // GATE_LANES):
        lanes = slice(GATE_LANES * lc, GATE_LANES * (lc + 1))
        for ii0 in range(0, I1_TILE, na):
            for rg0 in range(0, n_rg, nb):
                accs = [[jnp.zeros((GATE_ROWS, GATE_LANES), BF16) for _ in range(nb)] for _ in range(na)]
                for h in range(PEER_HEADS):
                    rk = [rank2_ref[h, GATE_ROWS * (rg0 + b):GATE_ROWS * (rg0 + b + 1), lanes] for b in range(nb)]
                    g2 = [g2_ref[h, GATE_ROWS * (rg0 + b):GATE_ROWS * (rg0 + b + 1), lanes] for b in range(nb)]
                    for a in range(na):
                        c = cb_ref[h, GATE_ROWS * (ii0 + a):GATE_ROWS * (ii0 + a + 1), lanes]
                        g = gb_ref[h, GATE_ROWS * (ii0 + a):GATE_ROWS * (ii0 + a + 1), lanes]
                        for b in range(nb):
                            accs[a][b] = accs[a][b] + jnp.where(rk[b] < c, g2[b] * g, jnp.zeros_like(g))
                for a in range(na):
                    for b in range(nb):
                        r0 = N_KEYS * (ii0 + a) + GATE_ROWS * (rg0 + b)
                        w_ref[r0:r0 + GATE_ROWS, lanes] = accs[a][b]

    hq = hq_ref[...]
    for rc in range(w_ref.shape[0] // ACT_ROWS):
        rows = slice(ACT_ROWS * rc, ACT_ROWS * (rc + 1))
        act = _dot_tb(eu_ref[rows, :], hq)
        gel = act * (lax.erf(act * (1.0 / math.sqrt(2.0))) + 1.0) * 0.5
        w_ref[rows, :] = w_ref[rows, :] * gel.astype(BF16)
    acc_ref[...] += _dot(evt_ref[...], w_ref[...])

    @pl.when(j == pl.num_programs(1) - 1)
    def _():
        o_ref[...] = acc_ref[...].T


def _experts(hq, eu, evt, cnt1, g1, rank2, g2, tm):
    n = hq.shape[0]
    n_exp = eu.shape[0]
    te = I1_TILE * N_KEYS
    assert tm % GATE_LANES == 0 and te % ACT_ROWS == 0 and n % tm == 0
    return pl.pallas_call(
        _experts_kernel,
        out_shape=jax.ShapeDtypeStruct((n, D), F32),
        grid=(n // tm, n_exp // te),
        in_specs=[pl.BlockSpec((tm, D), lambda i, j: (i, 0)),
                  pl.BlockSpec((te, D), lambda i, j: (j, 0)),
                  pl.BlockSpec((D, te), lambda i, j: (0, j)),
                  pl.BlockSpec((PEER_HEADS, I1_TILE, tm), lambda i, j: (0, j, i)),
                  pl.BlockSpec((PEER_HEADS, I1_TILE, tm), lambda i, j: (0, j, i)),
                  pl.BlockSpec((PEER_HEADS, N_KEYS, tm), lambda i, j: (0, 0, i)),
                  pl.BlockSpec((PEER_HEADS, N_KEYS, tm), lambda i, j: (0, 0, i))],
        out_specs=pl.BlockSpec((tm, D), lambda i, j: (i, 0)),
        scratch_shapes=[pltpu.VMEM((D, tm), F32), pltpu.VMEM((te, tm), BF16),
                        pltpu.VMEM((PEER_HEADS, I1_TILE * GATE_ROWS, tm), BF16),
                        pltpu.VMEM((PEER_HEADS, I1_TILE * GATE_ROWS, tm), BF16)],
        compiler_params=_cparams(("arbitrary", "arbitrary")),
        name="peer_experts",
    )(hq, eu, evt, cnt1, g1, rank2, g2)


def _pw1_kernel(x_ref, pe_ref, modp_ref, g_ref, mod_ref, wa_ref, wb_ref, ba_ref, bb_ref, o_ref, x1_ref):
    x1 = x_ref[...] + _mod_part(modp_ref[0], 5) * pe_ref[...]
    x1_ref[...] = x1
    h = _norm_mod(x1, g_ref[...], mod_ref[0], 0, 1).astype(BF16)
    a = _dot(h, wa_ref[...]) + ba_ref[...]
    b = _dot(h, wb_ref[...]) + bb_ref[...]
    o_ref[...] = a * jax.nn.sigmoid(b)


def _pw1(x, pe, mod_prev, g, mod, wa, wb, ba, bb, rowmap, tm):
    n = x.shape[0]
    c2 = lambda i: (0, 0)
    tok = pl.BlockSpec((tm, D), lambda i: (i, 0))
    return pl.pallas_call(
        _pw1_kernel,
        out_shape=(jax.ShapeDtypeStruct((n, D), F32), jax.ShapeDtypeStruct((n, D), F32)),
        grid=(n // tm,),
        in_specs=[tok, tok,
                  pl.BlockSpec((1, 1, 6 * D), rowmap),
                  pl.BlockSpec((1, D), c2),
                  pl.BlockSpec((1, 1, 6 * D), rowmap),
                  pl.BlockSpec((D, D), c2), pl.BlockSpec((D, D), c2),
                  pl.BlockSpec((1, D), c2), pl.BlockSpec((1, D), c2)],
        out_specs=(tok, tok),
        compiler_params=_cparams(("arbitrary",)),
        name="conv_pw1_glu",
    )(x, pe, mod_prev, g, mod, wa, wb, ba, bb)


def _conv_kernel(n_ctx_blocks, seg_ctx, seg_lat,
                 z_ref, x_ref, mod_ref, ck_ref, cb_ref, lg_ref, lb_ref, w2_ref, b2_ref, o_ref, pad_ref):
    tm = z_ref.shape[0]
    i = pl.program_id(0)
    seg = jnp.where(i < n_ctx_blocks, seg_ctx, seg_lat)
    zeros = jnp.zeros((CONV_PAD, D), F32)
    pad_ref[0:CONV_PAD, :] = zeros
    pad_ref[CONV_PAD + tm:, :] = zeros
    pad_ref[CONV_PAD:CONV_PAD + tm, :] = z_ref[...]
    pos = jnp.bitwise_and(lax.broadcasted_iota(jnp.int32, (tm, D), 0), seg - 1)
    acc = jnp.zeros((tm, D), F32)
    for k in range(CONV_W):
        off = k - CONV_HALF
        xs = pad_ref[CONV_PAD + off:CONV_PAD + off + tm, :]
        if off < 0:
            xs = jnp.where(pos >= -off, xs, 0.0)
        elif off > 0:
            xs = jnp.where(pos < seg - off, xs, 0.0)
        acc = acc + xs * ck_ref[k:k + 1, :]
    z = acc + cb_ref[...]
    m = jnp.mean(z, axis=-1, keepdims=True)
    dv = z - m
    var = jnp.mean(dv * dv, axis=-1, keepdims=True)
    zn = dv * lax.rsqrt(var + LN_EPS) * lg_ref[...] + lb_ref[...]
    act = (zn * jax.nn.sigmoid(zn)).astype(BF16)
    y = _dot(act, w2_ref[...]) + b2_ref[...]
    o_ref[...] = x_ref[...] + _mod_part(mod_ref[0], 2) * y


def _conv(z, x, mod, ck, cb, lg, lb, w2, b2, rowmap, tm, n_ctx_blocks, seg_ctx, seg_lat):
    n = x.shape[0]
    c2 = lambda i: (0, 0)
    return pl.pallas_call(
        functools.partial(_conv_kernel, n_ctx_blocks, seg_ctx, seg_lat),
        out_shape=jax.ShapeDtypeStruct((n, D), F32),
        grid=(n // tm,),
        in_specs=[pl.BlockSpec((tm, D), lambda i: (i, 0)),
                  pl.BlockSpec((tm, D), lambda i: (i, 0)),
                  pl.BlockSpec((1, 1, 6 * D), rowmap),
                  pl.BlockSpec(ck.shape, c2),
                  pl.BlockSpec((1, D), c2), pl.BlockSpec((1, D), c2), pl.BlockSpec((1, D), c2),
                  pl.BlockSpec((D, D), c2), pl.BlockSpec((1, D), c2)],
        out_specs=pl.BlockSpec((tm, D), lambda i: (i, 0)),
        scratch_shapes=[pltpu.VMEM((tm + 2 * CONV_PAD, D), F32)],
        compiler_params=_cparams(("arbitrary",)),
        name="conv_dw_ln_pw2",
    )(z, x, mod, ck, cb, lg, lb, w2, b2)


def _final_kernel(x_ref, pe_ref, mod_ref, g_ref, o_ref):
    x = x_ref[...] + _mod_part(mod_ref[0], 5) * pe_ref[...]
    ms = jnp.mean(x * x, axis=-1, keepdims=True)
    o_ref[...] = x * lax.rsqrt(ms + EPS) * g_ref[...]


def _final_norm(x, pe, mod, g, rowmap, tm):
    n = x.shape[0]
    tok = pl.BlockSpec((tm, D), lambda i: (i, 0))
    return pl.pallas_call(
        _final_kernel,
        out_shape=jax.ShapeDtypeStruct((n, D), F32),
        grid=(n // tm,),
        in_specs=[tok, tok, pl.BlockSpec((1, 1, 6 * D), rowmap), pl.BlockSpec((1, D), lambda i: (0, 0))],
        out_specs=tok,
        compiler_params=_cparams(("arbitrary",)),
        name="final_norm",
    )(x, pe, mod, g)


def _peer_layer(x, l, mod_l, norm2, w_query, sub_keys, expert_u, expert_v, rowmap_of):
    hq, q = _query(x, norm2[l][None, :], mod_l, w_query[l].astype(BF16), rowmap_of(512), 512)
    cnt1, g1, rank2, g2 = _topk(q, sub_keys[l], 512)
    eu = expert_u[l].astype(BF16)
    evt = expert_v[l].astype(BF16).T
    return _experts(hq, eu, evt, cnt1, g1, rank2, g2, 1024)


def kernel(x_prompt, x_sample, state_rwkv, c, c_ctx, w_mod, b_mod, norm1, norm2, norm_f, w_in_ab, mu_a, w0, w2, a0, a2, g2, k_k, k_a, r_k, lnx_g, lnx_b, lnv_g, lnv_b, w_s, b_s, w_out_ab, w_pw1, b_pw1, conv_k, conv_b, lnc_g, lnc_b, w_pw2, b_pw2, w_query, sub_keys, expert_u, expert_v):
    n_ctx, ctx_len, _ = x_prompt.shape
    n_lat, lat_len, _ = x_sample.shape
    nc_tok = n_ctx * ctx_len
    grid_w = 64
    x = jnp.concatenate([x_prompt.reshape(nc_tok, D), x_sample.reshape(n_lat * lat_len, D)], axis=0)

    cvec8 = jnp.zeros((8, D), F32).at[0].set(c_ctx).at[1:1 + n_lat].set(c)
    mod = _modulation(cvec8, w_mod, b_mod)
    mods = [mod[l].reshape(8, 1, 6 * D) for l in range(mod.shape[0])]
    rowmap_of = lambda tm: _mod_row_map(nc_tok, lat_len, tm)

    w_in = w_in_ab[0]
    za, ub = _inproj(x, norm1[0][None, :], mods[0], w_in[:, :A_COLS].astype(BF16),
                     w_in[:, A_COLS:].astype(BF16), rowmap_of(512), 512)
    row = lambda a: a.reshape(1, -1)
    pad_lora = lambda w, off: jnp.zeros((2, 128, A_W), F32).at[0, :HD].set(w[0]).at[1, HD:].set(w[1])
    heads = jnp.arange(A_W) // HD
    hsum = (heads[:, None] == heads[None, :]).astype(BF16)
    rparams = (row(mu_a[0]), w0[0], pad_lora(w2[0], 0), a0[0], pad_lora(a2[0], 0), g2[0],
               row(k_k[0]), row(k_a[0]), row(r_k[0]), row(lnx_g[0]), row(lnx_b[0]), hsum)
    zero_state = jnp.zeros((n_ctx, 2 * NPAIR, 128, 128), F32)
    ya_c, sfin = _rwkv(za, zero_state, ctx_len, 0, rparams)
    ya_s, _ = _rwkv(za, _states_to_blockdiag(state_rwkv[:, 0]), lat_len, nc_tok, rparams)
    ya = jnp.concatenate([ya_c, ya_s], axis=0)
    bsb = jnp.broadcast_to(b_s[0][:, :, None], (b_s.shape[1], GD, GD))
    x = _post(x, ya, ub, mods[0], row(lnv_g[0]), row(lnv_b[0]), w_s[0], bsb,
              w_out_ab[0].astype(BF16), rowmap_of(512), 512)
    pe = _peer_layer(x, 0, mods[0], norm2, w_query, sub_keys, expert_u, expert_v, rowmap_of)

    glu, x = _pw1(x, pe, mods[0], norm1[1][None, :], mods[1], w_pw1[0][:, :D].astype(BF16),
                  w_pw1[0][:, D:].astype(BF16), row(b_pw1[0][:D]), row(b_pw1[0][D:]), rowmap_of(512), 512)
    ck = jnp.zeros((32, D), F32).at[:CONV_W].set(conv_k[0])
    x = _conv(glu, x, mods[1], ck, row(conv_b[0]), row(lnc_g[0]), row(lnc_b[0]),
              w_pw2[0].astype(BF16), row(b_pw2[0]), rowmap_of(256), 256,
              nc_tok // 256, ctx_len, grid_w)
    pe = _peer_layer(x, 1, mods[1], norm2, w_query, sub_keys, expert_u, expert_v, rowmap_of)

    y = _final_norm(x, pe, mods[1], norm_f[None, :], rowmap_of(512), 512)
    y_prompt = y[:nc_tok].reshape(n_ctx, ctx_len, D)
    y_sample = y[nc_tok:].reshape(n_lat, lat_len, D)
    new_state = _blockdiag_to_states(sfin)[:, None].astype(x_prompt.dtype)
    return (y_prompt, y_sample, new_state)
```

```python
import functools
import math

import jax
import jax.numpy as jnp
from jax import lax
from jax.experimental import pallas as pl
from jax.experimental.pallas import tpu as pltpu

F32 = jnp.float32
BF16 = jnp.bfloat16

D = 1024
HD = 64
CH = 64
A_W = 512
NPAIR = A_W // 128
R_OFF, K_OFF, V_OFF = 0, 512, 1024
WLO_OFF, ALO_OFF, GLO_OFF, A_COLS = 1536, 1664, 1792, 1920
B_W = 512
GD = 128
CONV_W = 31
CONV_HALF = 15
CONV_PAD = 16
SUBLANES = 8
N_KEYS = 128
TOPK = 16
PEER_HEADS = 8
EPS = 1e-6
LN_EPS = 1e-5
GN_EPS = 64e-5
VMEM_LIMIT = 56 * 1024 * 1024


def _cparams(sem):
    return pltpu.CompilerParams(dimension_semantics=sem, vmem_limit_bytes=VMEM_LIMIT)


def _dot(a, b):
    return jnp.dot(a, b, preferred_element_type=F32)


def _dot_tb(a, b):
    return lax.dot_general(a, b, (((1,), (1,)), ((), ())), preferred_element_type=F32)


def _dot_ta(a, b):
    return lax.dot_general(a, b, (((0,), (0,)), ((), ())), preferred_element_type=F32)


def _split2(x):
    hi = x.astype(BF16)
    lo = (x - hi.astype(F32)).astype(BF16)
    return hi, lo


def _split3(x):
    hi = x.astype(BF16)
    r1 = x - hi.astype(F32)
    mid = r1.astype(BF16)
    lo = (r1 - mid.astype(F32)).astype(BF16)
    return hi, mid, lo


def _mm3(a, b, dot=_dot):
    ah, al = _split2(a)
    bh, bl = _split2(b)
    return dot(ah, bh) + dot(ah, bl) + dot(al, bh)


def _stack_k(p, axis, lhs):
    hi, lo = p
    return jnp.concatenate([hi, lo] if lhs else [hi, hi], axis=axis)


def _mmk(a_p, b_p):
    return _dot(_stack_k(a_p, 1, True), _stack_k(b_p, 0, False))


def _mmk_tb(a_p, b_p):
    return _dot_tb(_stack_k(a_p, 1, True), _stack_k(b_p, 1, False))


def _mmk_ta(a_p, b_p):
    return _dot_ta(_stack_k(a_p, 0, True), _stack_k(b_p, 0, False))


def _mm_exact_lhs(m_bf16, x):
    hi, mid, lo = _split3(x)
    return _dot(m_bf16, hi) + _dot(m_bf16, mid) + _dot(m_bf16, lo)


def _mm_exact_rhs(x, m_bf16):
    hi, mid, lo = _split3(x)
    return _dot(hi, m_bf16) + _dot(mid, m_bf16) + _dot(lo, m_bf16)


def _mod_part(mod_row, j):
    return mod_row[:, j * D:(j + 1) * D]


def _norm_mod(x, g, mod_row, j_shift, j_scale):
    ms = jnp.mean(x * x, axis=-1, keepdims=True)
    y = x * lax.rsqrt(ms + EPS) * g
    return y * (1.0 + _mod_part(mod_row, j_scale)) + _mod_part(mod_row, j_shift)


def _mod_row_map(n_ctx_tokens, seq_tokens, tm):
    ncb = n_ctx_tokens // tm
    per = seq_tokens // tm

    def imap(i, *_):
        return (jnp.where(i < ncb, 0, 1 + (i - ncb) // per), 0, 0)
    return imap


def _mod_kernel(c_ref, w_ref, b_ref, o_ref):
    c = c_ref[...]
    s = c * jax.nn.sigmoid(c)
    o_ref[0] = _mm3(s, w_ref[0]) + b_ref[0]


def _modulation(cvec8, w_mod, b_mod):
    depth, _, n6 = w_mod.shape
    tn = 1536
    return pl.pallas_call(
        _mod_kernel,
        out_shape=jax.ShapeDtypeStruct((depth, 8, n6), F32),
        grid=(depth, n6 // tn),
        in_specs=[pl.BlockSpec((8, D), lambda l, j: (0, 0)),
                  pl.BlockSpec((1, D, tn), lambda l, j: (l, 0, j)),
                  pl.BlockSpec((1, 1, tn), lambda l, j: (l, 0, j))],
        out_specs=pl.BlockSpec((1, 8, tn), lambda l, j: (l, 0, j)),
        compiler_params=_cparams(("arbitrary", "arbitrary")),
        name="modulation",
    )(cvec8, w_mod, b_mod.reshape(depth, 1, n6))


def _inproj_kernel(x_ref, g_ref, mod_ref, wa_ref, wb_ref, za_ref, ub_ref):
    h = _norm_mod(x_ref[...], g_ref[...], mod_ref[0], 0, 1).astype(BF16)
    za_ref[...] = _dot(h, wa_ref[...])
    ub_ref[...] = _dot(h, wb_ref[...])


def _inproj(x, g, mod, wa, wb, rowmap, tm):
    n = x.shape[0]
    return pl.pallas_call(
        _inproj_kernel,
        out_shape=(jax.ShapeDtypeStruct((n, A_COLS), F32), jax.ShapeDtypeStruct((n, 2 * B_W), F32)),
        grid=(n // tm,),
        in_specs=[pl.BlockSpec((tm, D), lambda i: (i, 0)),
                  pl.BlockSpec((1, D), lambda i: (0, 0)),
                  pl.BlockSpec((1, 1, 6 * D), rowmap),
                  pl.BlockSpec((D, A_COLS), lambda i: (0, 0)),
                  pl.BlockSpec((D, 2 * B_W), lambda i: (0, 0))],
        out_specs=(pl.BlockSpec((tm, A_COLS), lambda i: (i, 0)),
                   pl.BlockSpec((tm, 2 * B_W), lambda i: (i, 0))),
        compiler_params=_cparams(("arbitrary",)),
        name="inproj",
    )(x, g, mod, wa, wb)


def _softplus(x):
    return jnp.maximum(x, 0.0) + jnp.log1p(jnp.exp(-jnp.abs(x)))


def _rwkv_kernel(za_ref, s0_ref, mu_ref, w0_ref, w2_ref, a0_ref, a2_ref, g2_ref, kk_ref, ka_ref,
                 rk_ref, lng_ref, lnb_ref, hsum_ref, ya_ref, sfin_ref,
                 prep_ref, gam_ref, st_ref, y_ref, bon_ref, gate_ref):
    t_len = za_ref.shape[0]
    nc = t_len // CH
    mu = mu_ref[...]
    hsum = hsum_ref[...]
    row_c = lax.broadcasted_iota(jnp.int32, (CH, A_COLS), 0)
    ri = lax.broadcasted_iota(jnp.int32, (CH, CH), 0)
    ci = lax.broadcasted_iota(jnp.int32, (CH, CH), 1)
    tri_f = (ci <= ri).astype(BF16)
    tri_b = (ci >= ri).astype(BF16)
    r2 = lax.broadcasted_iota(jnp.int32, (2 * CH, 2 * CH), 0)
    c2 = lax.broadcasted_iota(jnp.int32, (2 * CH, 2 * CH), 1)
    r2m = jnp.bitwise_and(r2, CH - 1)
    c2m = jnp.bitwise_and(c2, CH - 1)
    eye2 = (r2 == c2).astype(F32)
    lane_lo = lax.broadcasted_iota(jnp.int32, (CH, 2 * HD), 1) < HD

    st_ref[...] = s0_ref[0]

    def stack(x):
        return jnp.concatenate([jnp.where(lane_lo, x, 0.0), jnp.where(lane_lo, 0.0, x)], axis=0)

    def mixed_chunk(c):
        start = pl.multiple_of(c * CH, CH)
        zc = za_ref[pl.ds(start, CH), :]
        p0 = pl.multiple_of(jnp.maximum(start - 8, 0), 8)
        n0 = pl.multiple_of(jnp.minimum(start + CH, t_len - 8), 8)
        prow = za_ref[pl.ds(p0, 8), :][7:8, :] * (start > 0).astype(F32)
        nrow = za_ref[pl.ds(n0, 8), :][0:1, :] * (start + CH < t_len).astype(F32)
        prev = jnp.where(row_c == 0, prow, pltpu.roll(zc, 1, 0))
        nxt = jnp.where(row_c == CH - 1, nrow, pltpu.roll(zc, CH - 1, 0))
        return zc + mu * (0.5 * (prev + nxt) - zc), start

    def prep(c, d):
        zm, start = mixed_chunk(c)
        r = zm[:, R_OFF:R_OFF + A_W]
        k = zm[:, K_OFF:K_OFF + A_W]
        v = zm[:, V_OFF:V_OFF + A_W]
        wlo2 = zm[:, WLO_OFF:WLO_OFF + 128]
        alo2 = zm[:, ALO_OFF:ALO_OFF + 128]
        kkr = k * kk_ref[...]
        n2 = _mm_exact_rhs(kkr * kkr, hsum)
        kk = kkr / jnp.maximum(jnp.sqrt(n2), 1e-12)
        wl = w0_ref[d:d + 1, :] + _mm3(jnp.tanh(wlo2), w2_ref[d])
        ld = -jnp.exp(-_softplus(-wl) - 0.5)
        ag = jax.nn.sigmoid(a0_ref[d:d + 1, :] + _mm3(alo2, a2_ref[d]))
        kd = k * (1.0 + (ag - 1.0) * ka_ref[...])
        bv = kk * ag
        cum = _mm_exact_lhs(tri_f if d == 0 else tri_b, ld)
        gam = jnp.exp(cum)
        gam_ex = jnp.exp(cum - ld)
        igam = jnp.exp(-cum)
        ops = (-kk * gam_ex, r * gam, bv * igam, kd * igam, v)
        for j, o in enumerate(ops):
            for p in range(NPAIR):
                prep_ref[d, j, p] = o[:, 128 * p:128 * (p + 1)]
        gtot = gam[CH - 1:CH, :] if d == 0 else gam[0:1, :]
        for p in range(NPAIR):
            gam_ref[d, p] = jnp.broadcast_to(gtot[:, 128 * p:128 * (p + 1)], (8, 128))
        bon_ref[d, pl.ds(start, CH), :] = _mm_exact_rhs(r * kd * rk_ref[...], hsum) * v
        if d == 0:
            glo = zm[:, GLO_OFF:GLO_OFF + 128]
            gate_ref[pl.ds(start, CH), :] = _mm3(jax.nn.sigmoid(glo), g2_ref[...])

    def solve_all(i):
        n2 = 2 * CH
        ps = range(2 * NPAIR)
        dq = [q // NPAIR for q in ps]
        pq = [q % NPAIR for q in ps]
        start = [pl.multiple_of(i * CH, CH), pl.multiple_of((nc - 1 - i) * CH, CH)]
        order = [r2m - c2m, c2m - r2m]
        a_s = [stack(prep_ref[dq[p], 0, pq[p]]) for p in ps]
        r_s = [stack(prep_ref[dq[p], 1, pq[p]]) for p in ps]
        b_p = [_split2(stack(prep_ref[dq[p], 2, pq[p]])) for p in ps]
        k_p = [_split2(stack(prep_ref[dq[p], 3, pq[p]])) for p in ps]
        v_p = [_split2(stack(prep_ref[dq[p], 4, pq[p]])) for p in ps]
        sc = [_mmk_tb(_split2(jnp.concatenate([a_s[p], r_s[p]], axis=0)),
                      tuple(jnp.concatenate([x, y], axis=0) for x, y in zip(b_p[p], k_p[p]))) for p in ps]
        sab = [jnp.where(order[dq[p]] > 0, sc[p][:n2, :n2], 0.0) for p in ps]
        sak = [jnp.where(order[dq[p]] > 0, sc[p][:n2, n2:], 0.0) for p in ps]
        nrb = [jnp.where(order[dq[p]] >= 0, sc[p][n2:, :n2], 0.0) for p in ps]
        nrk = [jnp.where(order[dq[p]] >= 0, sc[p][n2:, n2:], 0.0) for p in ps]
        pw = []
        for p in ps:
            pw_p = _split2(sab[p])
            pw.append(_mmk(pw_p, pw_p))
        tinv = [eye2 + sab[p] for p in ps]
        n_dbl = int(math.log2(CH))
        for kk in range(1, n_dbl):
            for p in ps:
                pw_p = _split2(pw[p])
                if kk < n_dbl - 1:
                    both = _mmk(pw_p, _split2(jnp.concatenate([pw[p], tinv[p]], axis=1)))
                    pw[p] = both[:, :n2]
                    tinv[p] = tinv[p] + both[:, n2:]
                else:
                    tinv[p] = tinv[p] + _mmk(pw_p, _split2(tinv[p]))
        sakv = [_mmk(_split2(sak[p]), v_p[p]) for p in ps]
        au_p = [_split2(_mmk(_split2(tinv[p]), _split2(jnp.concatenate([a_s[p], sakv[p]], axis=1))))
                for p in ps]
        ry = [_mmk(_split2(nrb[p]), au_p[p]) for p in ps]
        nv = [_mmk(_split2(nrk[p]), v_p[p]) for p in ps]
        gh = [_mmk_ta(au_p[p], b_p[p]) for p in ps]
        vk = [_mmk_ta(v_p[p], k_p[p]) for p in ps]
        s_p = [_split2(st_ref[p]) for p in ps]
        ys = [_mmk_tb(_split2(r_s[p] + ry[p][:, :n2]), s_p[p]) for p in ps]
        for p in ps:
            gt = gam_ref[dq[p], pq[p]][0:1, :]
            g_t = (eye2 + gh[p][:n2, :]) * gt
            h_t = (gh[p][n2:, :] + vk[p]) * gt
            y_all = ys[p] + ry[p][:, n2:] + nv[p]
            y_ref[dq[p], pq[p], pl.ds(start[dq[p]], CH), :] = y_all[:CH, :] + y_all[CH:, :]
            st_ref[p] = _mmk(s_p[p], _split2(g_t)) + h_t

    def chunk_body(i, carry):
        prep(i, 0)
        prep(nc - 1 - i, 1)
        solve_all(i)
        return carry

    lax.fori_loop(0, nc, chunk_body, 0)

    def out_body(c, carry):
        start = pl.multiple_of(c * CH, CH)
        rows = pl.ds(start, CH)
        ysum = jnp.concatenate([y_ref[0, p, rows, :] + y_ref[1, p, rows, :] for p in range(NPAIR)], axis=1)
        mean = _mm_exact_rhs(ysum, hsum) * (1.0 / HD)
        dev = ysum - mean
        var = _mm_exact_rhs(dev * dev, hsum) * (1.0 / HD)
        gn = dev * lax.rsqrt(var + GN_EPS) * lng_ref[...] + lnb_ref[...]
        bonus = bon_ref[0, rows, :] + bon_ref[1, rows, :]
        ya_ref[rows, :] = ((gn + bonus) * gate_ref[rows, :]).astype(ya_ref.dtype)
        return carry

    lax.fori_loop(0, nc, out_body, 0)
    sfin_ref[0] = st_ref[...]


def _rwkv(za, s0bd, t_len, first_token, params):
    n_seq = s0bd.shape[0]
    blk0 = first_token // t_len
    nq = 2 * NPAIR
    full = lambda a: pl.BlockSpec(a.shape, lambda s, _n=a.ndim: (0,) * _n)
    return pl.pallas_call(
        _rwkv_kernel,
        out_shape=(jax.ShapeDtypeStruct((n_seq * t_len, A_W), BF16),
                   jax.ShapeDtypeStruct((n_seq, nq, 128, 128), F32)),
        grid=(n_seq,),
        in_specs=[pl.BlockSpec((t_len, A_COLS), lambda s: (s + blk0, 0)),
                  pl.BlockSpec((1, nq, 128, 128), lambda s: (s, 0, 0, 0))] + [full(a) for a in params],
        out_specs=(pl.BlockSpec((t_len, A_W), lambda s: (s, 0)),
                   pl.BlockSpec((1, nq, 128, 128), lambda s: (s, 0, 0, 0))),
        scratch_shapes=[pltpu.VMEM((2, 5, NPAIR, CH, 128), F32),
                        pltpu.VMEM((2, NPAIR, 8, 128), F32),
                        pltpu.VMEM((nq, 128, 128), F32),
                        pltpu.VMEM((2, NPAIR, t_len, 128), F32),
                        pltpu.VMEM((2, t_len, A_W), F32),
                        pltpu.VMEM((t_len, A_W), F32)],
        compiler_params=_cparams(("arbitrary",)),
        name="rwkv_t%d" % t_len,
    )(za, s0bd, *params)


def _states_to_blockdiag(s):
    n = s.shape[0]
    s = s.reshape(n, 2, NPAIR, 2, HD, HD)
    z = jnp.zeros_like(s[:, :, :, 0])
    top = jnp.concatenate([s[:, :, :, 0], z], axis=-1)
    bot = jnp.concatenate([z, s[:, :, :, 1]], axis=-1)
    return jnp.concatenate([top, bot], axis=-2).reshape(n, 2 * NPAIR, 128, 128)


def _blockdiag_to_states(b):
    n = b.shape[0]
    b = b.reshape(n, 2, NPAIR, 128, 128)
    s = jnp.stack([b[..., :HD, :HD], b[..., HD:, HD:]], axis=3)
    return s.reshape(n, 2, 2 * NPAIR, HD, HD)


def _post_kernel(x_ref, ya_ref, ub_ref, mod_ref, lvg_ref, lvb_ref, ws_ref, bs_ref, wo_ref, o_ref):
    tm = x_ref.shape[0]
    ub = ub_ref[...]
    parts = []
    for g in range(B_W // GD):
        u = ub[:, GD * g:GD * (g + 1)]
        vb = ub[:, B_W + GD * g:B_W + GD * (g + 1)]
        m = jnp.mean(vb, axis=-1, keepdims=True)
        dv = vb - m
        var = jnp.mean(dv * dv, axis=-1, keepdims=True)
        vn = dv * lax.rsqrt(var + LN_EPS) * lvg_ref[:, GD * g:GD * (g + 1)] + lvb_ref[:, GD * g:GD * (g + 1)]
        sp = jnp.concatenate(
            [_mm3(ws_ref[g], vn[GD * c:GD * (c + 1), :]) + bs_ref[g] for c in range(tm // GD)], axis=0)
        parts.append((u * sp).astype(BF16))
    y = jnp.concatenate([ya_ref[...]] + parts, axis=1)
    o_ref[...] = x_ref[...] + _mod_part(mod_ref[0], 2) * _dot(y, wo_ref[...])


def _post(x, ya, ub, mod, lvg, lvb, ws, bsb, wo, rowmap, tm):
    n = x.shape[0]
    c2 = lambda i: (0, 0)
    c3 = lambda i: (0, 0, 0)
    return pl.pallas_call(
        _post_kernel,
        out_shape=jax.ShapeDtypeStruct((n, D), F32),
        grid=(n // tm,),
        in_specs=[pl.BlockSpec((tm, D), lambda i: (i, 0)),
                  pl.BlockSpec((tm, A_W), lambda i: (i, 0)),
                  pl.BlockSpec((tm, 2 * B_W), lambda i: (i, 0)),
                  pl.BlockSpec((1, 1, 6 * D), rowmap),
                  pl.BlockSpec((1, B_W), c2), pl.BlockSpec((1, B_W), c2),
                  pl.BlockSpec(ws.shape, c3), pl.BlockSpec(bsb.shape, c3),
                  pl.BlockSpec((D, D), c2)],
        out_specs=pl.BlockSpec((tm, D), lambda i: (i, 0)),
        compiler_params=_cparams(("arbitrary",)),
        name="gmlp_outproj",
    )(x, ya, ub, mod, lvg, lvb, ws, bsb, wo)


def _query_kernel(x_ref, g_ref, mod_ref, wq_ref, hq_ref, q_ref):
    h = _norm_mod(x_ref[...], g_ref[...], mod_ref[0], 3, 4).astype(BF16)
    hq_ref[...] = h
    q_ref[...] = _dot(h, wq_ref[...])


def _query(x, g, mod, wq, rowmap, tm):
    n = x.shape[0]
    nq = wq.shape[1]
    return pl.pallas_call(
        _query_kernel,
        out_shape=(jax.ShapeDtypeStruct((n, D), BF16), jax.ShapeDtypeStruct((n, nq), F32)),
        grid=(n // tm,),
        in_specs=[pl.BlockSpec((tm, D), lambda i: (i, 0)),
                  pl.BlockSpec((1, D), lambda i: (0, 0)),
                  pl.BlockSpec((1, 1, 6 * D), rowmap),
                  pl.BlockSpec((D, nq), lambda i: (0, 0))],
        out_specs=(pl.BlockSpec((tm, D), lambda i: (i, 0)), pl.BlockSpec((tm, nq), lambda i: (i, 0))),
        compiler_params=_cparams(("arbitrary",)),
        name="peer_query",
    )(x, g, mod, wq)


def _extract_top(s, rowid, n_rows, count):
    rank = jnp.full(s.shape, float(count), F32)
    vals = []
    for r in range(count):
        m = jnp.max(s, axis=0, keepdims=True)
        idx = jnp.min(jnp.where(s == m, rowid, float(n_rows)), axis=0, keepdims=True)
        hit = rowid == idx
        rank = jnp.where(hit, float(r), rank)
        s = jnp.where(hit, -jnp.inf, s)
        vals.append(m)
    return rank, vals


_CAND_Q = [TOPK // (p + 1) for p in range(TOPK)]


def _pull_distinct(s, count, want_rank=False):
    vals = []
    rank = jnp.full(s.shape, float(count), F32) if want_rank else None
    for r in range(count):
        m = jnp.max(s, axis=0, keepdims=True)
        hit = s == m
        if want_rank:
            rank = jnp.where(hit, float(r), rank)
        s = jnp.where(hit, -jnp.inf, s)
        vals.append(m)
    return (vals, rank) if want_rank else vals


def _candidates(a, b, tb):
    bmat = jnp.concatenate(b, axis=0)
    amat = jnp.concatenate(a, axis=0)
    row8 = lax.broadcasted_iota(jnp.int32, (8, tb), 0)
    blocks = [a[0] + bmat]
    for p in range(1, 8):
        blocks.append(jnp.where(row8 < _CAND_Q[p], a[p] + bmat[:8, :], -jnp.inf))
    blocks.append(amat[8:, :] + b[0])
    return jnp.concatenate(blocks, axis=0)


def _cand_counts(self32):
    cnt = [jnp.sum(self32[0:16], axis=0, keepdims=True)]
    for p in range(1, 8):
        cnt.append(jnp.sum(self32[8 + 8 * p:16 + 8 * p], axis=0, keepdims=True))
    for p in range(8, 16):
        cnt.append(self32[72 + (p - 8):73 + (p - 8)])
    return cnt


def _topk_kernel(q_ref, sk_ref, cnt1_ref, g1_ref, rank2_ref, g2_ref):
    tb = q_ref.shape[0]
    q = q_ref[...]
    s1 = _mm3(sk_ref[0], q[:, :N_KEYS], _dot_tb)
    s2 = _mm3(sk_ref[1], q[:, N_KEYS:], _dot_tb)

    a = _pull_distinct(s1, TOPK)
    b, rank2 = _pull_distinct(s2, TOPK, want_rank=True)
    cand = _candidates(a, b, tb)
    cthr = _pull_distinct(cand, TOPK)[TOPK - 1]
    sel = cand >= cthr
    top = a[0] + b[0]
    z = jnp.sum(jnp.where(sel, jnp.exp(cand - top), 0.0), axis=0, keepdims=True)
    self32 = sel.astype(F32)
    cnt = _cand_counts(self32)
    cnt1 = jnp.zeros((N_KEYS, tb), F32)
    for p in range(TOPK):
        cnt1 = jnp.where(s1 == a[p], cnt[p], cnt1)
    cnt1_ref[0] = cnt1
    g1_ref[0] = jnp.exp(s1 - a[0]) / z
    rank2_ref[0] = rank2.astype(rank2_ref.dtype)
    g2_ref[0] = jnp.exp(s2 - b[0]).astype(g2_ref.dtype)

    n1 = jnp.sum((s1 >= a[TOPK - 1]).astype(F32), axis=0, keepdims=True)
    n2 = jnp.sum((s2 >= b[TOPK - 1]).astype(F32), axis=0, keepdims=True)
    nc = jnp.sum(self32, axis=0, keepdims=True)
    bad = jnp.abs(n1 - TOPK) + jnp.abs(n2 - TOPK) + jnp.abs(nc - TOPK)

    @pl.when(jnp.max(bad) > 0.0)
    def _():
        rowid = lax.broadcasted_iota(jnp.int32, (N_KEYS, tb), 0).astype(F32)
        rank1, ax = _extract_top(s1, rowid, N_KEYS, TOPK)
        rank2x, bx = _extract_top(s2, rowid, N_KEYS, TOPK)
        candx = _candidates(ax, bx, tb)
        n_cand = candx.shape[0]
        crow = lax.broadcasted_iota(jnp.int32, (n_cand, tb), 0).astype(F32)
        crank, _ = _extract_top(candx, crow, n_cand, TOPK)
        selx = crank < float(TOPK)
        zx = jnp.sum(jnp.where(selx, jnp.exp(candx - top), 0.0), axis=0, keepdims=True)
        cntx = _cand_counts(selx.astype(F32))
        cnt1x = jnp.zeros((N_KEYS, tb), F32)
        for p in range(TOPK):
            cnt1x = jnp.where(rank1 == float(p), cntx[p], cnt1x)
        cnt1_ref[0] = cnt1x
        g1_ref[0] = jnp.exp(s1 - a[0]) / zx
        rank2_ref[0] = rank2x.astype(rank2_ref.dtype)


def _topk(q, sk, tb):
    n = q.shape[0]
    shp = jax.ShapeDtypeStruct((PEER_HEADS, N_KEYS, n), F32)
    shp16 = jax.ShapeDtypeStruct((PEER_HEADS, N_KEYS, n), BF16)
    ospec = pl.BlockSpec((1, N_KEYS, tb), lambda i, h: (h, 0, i))
    return pl.pallas_call(
        _topk_kernel,
        out_shape=(shp, shp, shp16, shp16),
        grid=(n // tb, PEER_HEADS),
        in_specs=[pl.BlockSpec((tb, 2 * N_KEYS), lambda i, h: (i, h)),
                  pl.BlockSpec((2, N_KEYS, N_KEYS), lambda i, h: (0, 0, 0))],
        out_specs=(ospec, ospec, ospec, ospec),
        compiler_params=_cparams(("arbitrary", "arbitrary")),
        name="peer_topk",
    )(q, sk)


I1_TILE = 8


GATE_LANES = 256
GATE_BLOCK = (2, 4)
GATE_ROWS = 16
ACT_ROWS = 256


def _experts_kernel(hq_ref, eu_ref, evt_ref, cnt1_ref, g1_ref, rank2_ref, g2_ref,
                    o_ref, acc_ref, w_ref, cb_ref, gb_ref):
    j = pl.program_id(1)
    tm = hq_ref.shape[0]

    @pl.when(j == 0)
    def _():
        acc_ref[...] = jnp.zeros_like(acc_ref)

    for h in range(PEER_HEADS):
        for ii in range(I1_TILE):
            rows = slice(GATE_ROWS * ii, GATE_ROWS * (ii + 1))
            cb_ref[h, rows, :] = jnp.broadcast_to(cnt1_ref[h, ii:ii + 1, :], (GATE_ROWS, tm)).astype(BF16)
            gb_ref[h, rows, :] = jnp.broadcast_to(g1_ref[h, ii:ii + 1, :], (GATE_ROWS, tm)).astype(BF16)

    n_rg = N_KEYS // GATE_ROWS
    na, nb = GATE_BLOCK
    for lc in range(tm // GATE_LANES):
        lanes = slice(GATE_LANES * lc, GATE_LANES * (lc + 1))
        for ii0 in range(0, I1_TILE, na):
            for rg0 in range(0, n_rg, nb):
                accs = [[jnp.zeros((GATE_ROWS, GATE_LANES), BF16) for _ in range(nb)] for _ in range(na)]
                for h in range(PEER_HEADS):
                    rk = [rank2_ref[h, GATE_ROWS * (rg0 + b):GATE_ROWS * (rg0 + b + 1), lanes] for b in range(nb)]
                    g2 = [g2_ref[h, GATE_ROWS * (rg0 + b):GATE_ROWS * (rg0 + b + 1), lanes] for b in range(nb)]
                    for a in range(na):
                        c = cb_ref[h, GATE_ROWS * (ii0 + a):GATE_ROWS * (ii0 + a + 1), lanes]
                        g = gb_ref[h, GATE_ROWS * (ii0 + a):GATE_ROWS * (ii0 + a + 1), lanes]
                        for b in range(nb):
                            accs[a][b] = accs[a][b] + jnp.where(rk[b] < c, g2[b] * g, jnp.zeros_like(g))
                for a in range(na):
                    for b in range(nb):
                        r0 = N_KEYS * (ii0 + a) + GATE_ROWS * (rg0 + b)
                        w_ref[r0:r0 + GATE_ROWS, lanes] = accs[a][b]

    hq = hq_ref[...]
    for rc in range(w_ref.shape[0] // ACT_ROWS):
        rows = slice(ACT_ROWS * rc, ACT_ROWS * (rc + 1))
        act = _dot_tb(eu_ref[rows, :], hq)
        gel = act * (lax.erf(act * (1.0 / math.sqrt(2.0))) + 1.0) * 0.5
        w_ref[rows, :] = w_ref[rows, :] * gel.astype(BF16)
    acc_ref[...] += _dot(evt_ref[...], w_ref[...])

    @pl.when(j == pl.num_programs(1) - 1)
    def _():
        o_ref[...] = acc_ref[...].T


def _experts(hq, eu, evt, cnt1, g1, rank2, g2, tm):
    n = hq.shape[0]
    n_exp = eu.shape[0]
    te = I1_TILE * N_KEYS
    assert tm % GATE_LANES == 0 and te % ACT_ROWS == 0 and n % tm == 0
    return pl.pallas_call(
        _experts_kernel,
        out_shape=jax.ShapeDtypeStruct((n, D), F32),
        grid=(n // tm, n_exp // te),
        in_specs=[pl.BlockSpec((tm, D), lambda i, j: (i, 0)),
                  pl.BlockSpec((te, D), lambda i, j: (j, 0)),
                  pl.BlockSpec((D, te), lambda i, j: (0, j)),
                  pl.BlockSpec((PEER_HEADS, I1_TILE, tm), lambda i, j: (0, j, i)),
                  pl.BlockSpec((PEER_HEADS, I1_TILE, tm), lambda i, j: (0, j, i)),
                  pl.BlockSpec((PEER_HEADS, N_KEYS, tm), lambda i, j: (0, 0, i)),
                  pl.BlockSpec((PEER_HEADS, N_KEYS, tm), lambda i, j: (0, 0, i))],
        out_specs=pl.BlockSpec((tm, D), lambda i, j: (i, 0)),
        scratch_shapes=[pltpu.VMEM((D, tm), F32), pltpu.VMEM((te, tm), BF16),
                        pltpu.VMEM((PEER_HEADS, I1_TILE * GATE_ROWS, tm), BF16),
                        pltpu.VMEM((PEER_HEADS, I1_TILE * GATE_ROWS, tm), BF16)],
        compiler_params=_cparams(("arbitrary", "arbitrary")),
        name="peer_experts",
    )(hq, eu, evt, cnt1, g1, rank2, g2)


def _pw1_kernel(x_ref, pe_ref, modp_ref, g_ref, mod_ref, wa_ref, wb_ref, ba_ref, bb_ref, o_ref, x1_ref):
    x1 = x_ref[...] + _mod_part(modp_ref[0], 5) * pe_ref[...]
    x1_ref[...] = x1
    h = _norm_mod(x1, g_ref[...], mod_ref[0], 0, 1).astype(BF16)
    a = _dot(h, wa_ref[...]) + ba_ref[...]
    b = _dot(h, wb_ref[...]) + bb_ref[...]
    o_ref[...] = a * jax.nn.sigmoid(b)


def _pw1(x, pe, mod_prev, g, mod, wa, wb, ba, bb, rowmap, tm):
    n = x.shape[0]
    c2 = lambda i: (0, 0)
    tok = pl.BlockSpec((tm, D), lambda i: (i, 0))
    return pl.pallas_call(
        _pw1_kernel,
        out_shape=(jax.ShapeDtypeStruct((n, D), F32), jax.ShapeDtypeStruct((n, D), F32)),
        grid=(n // tm,),
        in_specs=[tok, tok,
                  pl.BlockSpec((1, 1, 6 * D), rowmap),
                  pl.BlockSpec((1, D), c2),
                  pl.BlockSpec((1, 1, 6 * D), rowmap),
                  pl.BlockSpec((D, D), c2), pl.BlockSpec((D, D), c2),
                  pl.BlockSpec((1, D), c2), pl.BlockSpec((1, D), c2)],
        out_specs=(tok, tok),
        compiler_params=_cparams(("arbitrary",)),
        name="conv_pw1_glu",
    )(x, pe, mod_prev, g, mod, wa, wb, ba, bb)


def _conv_kernel(n_ctx_blocks, seg_ctx, seg_lat,
                 z_ref, x_ref, mod_ref, ck_ref, cb_ref, lg_ref, lb_ref, w2_ref, b2_ref, o_ref,
                 pad_ref, sh_ref, acc_ref):
    tm = z_ref.shape[0]
    i = pl.program_id(0)

    def dwconv(seg):
        nseg = tm // seg
        pitch = seg + CONV_PAD
        zeros = jnp.zeros((CONV_PAD, D), F32)
        for s in range(nseg + 1):
            pad_ref[pitch * s:pitch * s + CONV_PAD, :] = zeros
        for s in range(nseg):
            pad_ref[CONV_PAD + pitch * s:CONV_PAD + pitch * s + seg, :] = z_ref[seg * s:seg * (s + 1), :]
        used = pitch * nseg + CONV_PAD
        span = used - SUBLANES
        for ph in range(SUBLANES):
            sh_ref[ph, 0:span, :] = pad_ref[ph:ph + span, :]
        acc = None
        for k in range(CONV_W):
            row0 = CONV_PAD + k - CONV_HALF
            ph = row0 % SUBLANES
            base = row0 - ph
            parts = [sh_ref[ph, base + pitch * s:base + pitch * s + seg, :] for s in range(nseg)]
            xs = parts[0] if nseg == 1 else jnp.concatenate(parts, axis=0)
            term = xs * ck_ref[k:k + 1, :]
            acc = term if acc is None else acc + term
        acc_ref[...] = acc

    @pl.when(i < n_ctx_blocks)
    def _():
        dwconv(seg_ctx)

    @pl.when(i >= n_ctx_blocks)
    def _():
        dwconv(seg_lat)

    z = acc_ref[...] + cb_ref[...]
    m = jnp.mean(z, axis=-1, keepdims=True)
    dv = z - m
    var = jnp.mean(dv * dv, axis=-1, keepdims=True)
    zn = dv * lax.rsqrt(var + LN_EPS) * lg_ref[...] + lb_ref[...]
    act = (zn * jax.nn.sigmoid(zn)).astype(BF16)
    y = _dot(act, w2_ref[...]) + b2_ref[...]
    o_ref[...] = x_ref[...] + _mod_part(mod_ref[0], 2) * y


def _conv(z, x, mod, ck, cb, lg, lb, w2, b2, rowmap, tm, n_ctx_blocks, seg_ctx, seg_lat):
    n = x.shape[0]
    assert tm % seg_ctx == 0 and tm % seg_lat == 0 and CONV_PAD >= CONV_HALF and CONV_PAD % SUBLANES == 0
    pad_rows = tm + (tm // min(seg_ctx, seg_lat) + 1) * CONV_PAD
    c2 = lambda i: (0, 0)
    return pl.pallas_call(
        functools.partial(_conv_kernel, n_ctx_blocks, seg_ctx, seg_lat),
        out_shape=jax.ShapeDtypeStruct((n, D), F32),
        grid=(n // tm,),
        in_specs=[pl.BlockSpec((tm, D), lambda i: (i, 0)),
                  pl.BlockSpec((tm, D), lambda i: (i, 0)),
                  pl.BlockSpec((1, 1, 6 * D), rowmap),
                  pl.BlockSpec(ck.shape, c2),
                  pl.BlockSpec((1, D), c2), pl.BlockSpec((1, D), c2), pl.BlockSpec((1, D), c2),
                  pl.BlockSpec((D, D), c2), pl.BlockSpec((1, D), c2)],
        out_specs=pl.BlockSpec((tm, D), lambda i: (i, 0)),
        scratch_shapes=[pltpu.VMEM((pad_rows, D), F32),
                        pltpu.VMEM((SUBLANES, pad_rows, D), F32),
                        pltpu.VMEM((tm, D), F32)],
        compiler_params=_cparams(("arbitrary",)),
        name="conv_dw_ln_pw2",
    )(z, x, mod, ck, cb, lg, lb, w2, b2)


def _final_kernel(x_ref, pe_ref, mod_ref, g_ref, o_ref):
    x = x_ref[...] + _mod_part(mod_ref[0], 5) * pe_ref[...]
    ms = jnp.mean(x * x, axis=-1, keepdims=True)
    o_ref[...] = x * lax.rsqrt(ms + EPS) * g_ref[...]


def _final_norm(x, pe, mod, g, rowmap, tm):
    n = x.shape[0]
    tok = pl.BlockSpec((tm, D), lambda i: (i, 0))
    return pl.pallas_call(
        _final_kernel,
        out_shape=jax.ShapeDtypeStruct((n, D), F32),
        grid=(n // tm,),
        in_specs=[tok, tok, pl.BlockSpec((1, 1, 6 * D), rowmap), pl.BlockSpec((1, D), lambda i: (0, 0))],
        out_specs=tok,
        compiler_params=_cparams(("arbitrary",)),
        name="final_norm",
    )(x, pe, mod, g)


def _peer_layer(x, l, mod_l, norm2, w_query, sub_keys, expert_u, expert_v, rowmap_of):
    hq, q = _query(x, norm2[l][None, :], mod_l, w_query[l].astype(BF16), rowmap_of(512), 512)
    cnt1, g1, rank2, g2 = _topk(q, sub_keys[l], 512)
    eu = expert_u[l].astype(BF16)
    evt = expert_v[l].astype(BF16).T
    return _experts(hq, eu, evt, cnt1, g1, rank2, g2, 1024)


def kernel(x_prompt, x_sample, state_rwkv, c, c_ctx, w_mod, b_mod, norm1, norm2, norm_f, w_in_ab, mu_a, w0, w2, a0, a2, g2, k_k, k_a, r_k, lnx_g, lnx_b, lnv_g, lnv_b, w_s, b_s, w_out_ab, w_pw1, b_pw1, conv_k, conv_b, lnc_g, lnc_b, w_pw2, b_pw2, w_query, sub_keys, expert_u, expert_v):
    n_ctx, ctx_len, _ = x_prompt.shape
    n_lat, lat_len, _ = x_sample.shape
    nc_tok = n_ctx * ctx_len
    grid_w = 64
    x = jnp.concatenate([x_prompt.reshape(nc_tok, D), x_sample.reshape(n_lat * lat_len, D)], axis=0)

    cvec8 = jnp.zeros((8, D), F32).at[0].set(c_ctx).at[1:1 + n_lat].set(c)
    mod = _modulation(cvec8, w_mod, b_mod)
    mods = [mod[l].reshape(8, 1, 6 * D) for l in range(mod.shape[0])]
    rowmap_of = lambda tm: _mod_row_map(nc_tok, lat_len, tm)

    w_in = w_in_ab[0]
    za, ub = _inproj(x, norm1[0][None, :], mods[0], w_in[:, :A_COLS].astype(BF16),
                     w_in[:, A_COLS:].astype(BF16), rowmap_of(512), 512)
    row = lambda a: a.reshape(1, -1)
    pad_lora = lambda w, off: jnp.zeros((2, 128, A_W), F32).at[0, :HD].set(w[0]).at[1, HD:].set(w[1])
    heads = jnp.arange(A_W) // HD
    hsum = (heads[:, None] == heads[None, :]).astype(BF16)
    rparams = (row(mu_a[0]), w0[0], pad_lora(w2[0], 0), a0[0], pad_lora(a2[0], 0), g2[0],
               row(k_k[0]), row(k_a[0]), row(r_k[0]), row(lnx_g[0]), row(lnx_b[0]), hsum)
    zero_state = jnp.zeros((n_ctx, 2 * NPAIR, 128, 128), F32)
    ya_c, sfin = _rwkv(za, zero_state, ctx_len, 0, rparams)
    ya_s, _ = _rwkv(za, _states_to_blockdiag(state_rwkv[:, 0]), lat_len, nc_tok, rparams)
    ya = jnp.concatenate([ya_c, ya_s], axis=0)
    bsb = jnp.broadcast_to(b_s[0][:, :, None], (b_s.shape[1], GD, GD))
    x = _post(x, ya, ub, mods[0], row(lnv_g[0]), row(lnv_b[0]), w_s[0], bsb,
              w_out_ab[0].astype(BF16), rowmap_of(512), 512)
    pe = _peer_layer(x, 0, mods[0], norm2, w_query, sub_keys, expert_u, expert_v, rowmap_of)

    glu, x = _pw1(x, pe, mods[0], norm1[1][None, :], mods[1], w_pw1[0][:, :D].astype(BF16),
                  w_pw1[0][:, D:].astype(BF16), row(b_pw1[0][:D]), row(b_pw1[0][D:]), rowmap_of(512), 512)
    ck = jnp.zeros((32, D), F32).at[:CONV_W].set(conv_k[0])
    x = _conv(glu, x, mods[1], ck, row(conv_b[0]), row(lnc_g[0]), row(lnc_b[0]),
              w_pw2[0].astype(BF16), row(b_pw2[0]), rowmap_of(256), 256,
              nc_tok // 256, ctx_len, grid_w)
    pe = _peer_layer(x, 1, mods[1], norm2, w_query, sub_keys, expert_u, expert_v, rowmap_of)

    y = _final_norm(x, pe, mods[1], norm_f[None, :], rowmap_of(512), 512)
    y_prompt = y[:nc_tok].reshape(n_ctx, ctx_len, D)
    y_sample = y[nc_tok:].reshape(n_lat, lat_len, D)
    new_state = _blockdiag_to_states(sfin)[:, None].astype(x_prompt.dtype)
    return (y_prompt, y_sample, new_state)
```

```python
import functools
import math

import jax
import jax.numpy as jnp
from jax import lax
from jax.experimental import pallas as pl
from jax.experimental.pallas import tpu as pltpu

F32 = jnp.float32
BF16 = jnp.bfloat16

D = 1024
HD = 64
CH = 64
A_W = 512
NPAIR = A_W // 128
R_OFF, K_OFF, V_OFF = 0, 512, 1024
WLO_OFF, ALO_OFF, GLO_OFF, A_COLS = 1536, 1664, 1792, 1920
B_W = 512
GD = 128
CONV_W = 31
CONV_HALF = 15
CONV_PAD = 16
SUBLANES = 8
N_KEYS = 128
TOPK = 16
PEER_HEADS = 8
EPS = 1e-6
LN_EPS = 1e-5
GN_EPS = 64e-5
VMEM_LIMIT = 56 * 1024 * 1024


def _cparams(sem):
    return pltpu.CompilerParams(dimension_semantics=sem, vmem_limit_bytes=VMEM_LIMIT)


def _dot(a, b):
    return jnp.dot(a, b, preferred_element_type=F32)


def _dot_tb(a, b):
    return lax.dot_general(a, b, (((1,), (1,)), ((), ())), preferred_element_type=F32)


def _dot_ta(a, b):
    return lax.dot_general(a, b, (((0,), (0,)), ((), ())), preferred_element_type=F32)


def _split2(x):
    hi = x.astype(BF16)
    lo = (x - hi.astype(F32)).astype(BF16)
    return hi, lo


def _split3(x):
    hi = x.astype(BF16)
    r1 = x - hi.astype(F32)
    mid = r1.astype(BF16)
    lo = (r1 - mid.astype(F32)).astype(BF16)
    return hi, mid, lo


def _mm3(a, b, dot=_dot):
    ah, al = _split2(a)
    bh, bl = _split2(b)
    return dot(ah, bh) + dot(ah, bl) + dot(al, bh)


def _stack_k(p, axis, lhs):
    hi, lo = p
    return jnp.concatenate([hi, lo] if lhs else [hi, hi], axis=axis)


def _mmk(a_p, b_p):
    return _dot(_stack_k(a_p, 1, True), _stack_k(b_p, 0, False))


def _mmk_tb(a_p, b_p):
    return _dot_tb(_stack_k(a_p, 1, True), _stack_k(b_p, 1, False))


def _mmk_ta(a_p, b_p):
    return _dot_ta(_stack_k(a_p, 0, True), _stack_k(b_p, 0, False))


def _mm_exact_lhs(m_bf16, x):
    hi, mid, lo = _split3(x)
    return _dot(m_bf16, hi) + _dot(m_bf16, mid) + _dot(m_bf16, lo)


def _mm_exact_rhs(x, m_bf16):
    hi, lo = _split2(x)
    return _dot(hi, m_bf16) + _dot(lo, m_bf16)


def _mod_part(mod_row, j):
    return mod_row[:, j * D:(j + 1) * D]


def _norm_mod(x, g, mod_row, j_shift, j_scale):
    ms = jnp.mean(x * x, axis=-1, keepdims=True)
    y = x * lax.rsqrt(ms + EPS) * g
    return y * (1.0 + _mod_part(mod_row, j_scale)) + _mod_part(mod_row, j_shift)


def _mod_row_map(n_ctx_tokens, seq_tokens, tm):
    ncb = n_ctx_tokens // tm
    per = seq_tokens // tm

    def imap(i, *_):
        return (jnp.where(i < ncb, 0, 1 + (i - ncb) // per), 0, 0)
    return imap


def _mod_kernel(c_ref, w_ref, b_ref, o_ref):
    c = c_ref[...]
    s = c * jax.nn.sigmoid(c)
    o_ref[0] = _mm3(s, w_ref[0]) + b_ref[0]


def _modulation(cvec8, w_mod, b_mod):
    depth, _, n6 = w_mod.shape
    tn = 1536
    return pl.pallas_call(
        _mod_kernel,
        out_shape=jax.ShapeDtypeStruct((depth, 8, n6), F32),
        grid=(depth, n6 // tn),
        in_specs=[pl.BlockSpec((8, D), lambda l, j: (0, 0)),
                  pl.BlockSpec((1, D, tn), lambda l, j: (l, 0, j)),
                  pl.BlockSpec((1, 1, tn), lambda l, j: (l, 0, j))],
        out_specs=pl.BlockSpec((1, 8, tn), lambda l, j: (l, 0, j)),
        compiler_params=_cparams(("arbitrary", "arbitrary")),
        name="modulation",
    )(cvec8, w_mod, b_mod.reshape(depth, 1, n6))


def _stream_specs(n_ctx_tokens, tm):
    ncb = n_ctx_tokens // tm
    return (pl.BlockSpec((tm, D), lambda i: (jnp.minimum(i, ncb - 1), 0)),
            pl.BlockSpec((tm, D), lambda i: (jnp.maximum(i - ncb, 0), 0)))


def _stream_block(ncb, xc_ref, xl_ref):
    return jnp.where(pl.program_id(0) < ncb, xc_ref[...], xl_ref[...])


def _inproj_kernel(ncb, xc_ref, xl_ref, g_ref, mod_ref, wa_ref, wb_ref, za_ref, ub_ref):
    x = _stream_block(ncb, xc_ref, xl_ref)
    h = _norm_mod(x, g_ref[...], mod_ref[0], 0, 1).astype(BF16)
    za_ref[...] = _dot(h, wa_ref[...])
    ub_ref[...] = _dot(h, wb_ref[...])


def _inproj(xc, xl, g, mod, wa, wb, rowmap, tm):
    n = xc.shape[0] + xl.shape[0]
    return pl.pallas_call(
        functools.partial(_inproj_kernel, xc.shape[0] // tm),
        out_shape=(jax.ShapeDtypeStruct((n, A_COLS), F32), jax.ShapeDtypeStruct((n, 2 * B_W), F32)),
        grid=(n // tm,),
        in_specs=[*_stream_specs(xc.shape[0], tm),
                  pl.BlockSpec((1, D), lambda i: (0, 0)),
                  pl.BlockSpec((1, 1, 6 * D), rowmap),
                  pl.BlockSpec((D, A_COLS), lambda i: (0, 0)),
                  pl.BlockSpec((D, 2 * B_W), lambda i: (0, 0))],
        out_specs=(pl.BlockSpec((tm, A_COLS), lambda i: (i, 0)),
                   pl.BlockSpec((tm, 2 * B_W), lambda i: (i, 0))),
        compiler_params=_cparams(("arbitrary",)),
        name="inproj",
    )(xc, xl, g, mod, wa, wb)


def _softplus(x):
    return jnp.maximum(x, 0.0) + jnp.log1p(jnp.exp(-jnp.abs(x)))


def _rwkv_kernel(za_ref, s0_ref, mu_ref, w0_ref, w2_ref, a0_ref, a2_ref, g2_ref, kk_ref, ka_ref,
                 rk_ref, lng_ref, lnb_ref, hsum_ref, ya_ref, sfin_ref,
                 prep_ref, gam_ref, st_ref, y_ref, bon_ref, gate_ref):
    t_len = za_ref.shape[0]
    nc = t_len // CH
    mu = mu_ref[...]
    hsum = hsum_ref[...]
    row_c = lax.broadcasted_iota(jnp.int32, (CH, A_COLS), 0)
    ri = lax.broadcasted_iota(jnp.int32, (CH, CH), 0)
    ci = lax.broadcasted_iota(jnp.int32, (CH, CH), 1)
    tri_f = (ci <= ri).astype(BF16)
    tri_b = (ci >= ri).astype(BF16)
    r2 = lax.broadcasted_iota(jnp.int32, (2 * CH, 2 * CH), 0)
    c2 = lax.broadcasted_iota(jnp.int32, (2 * CH, 2 * CH), 1)
    r2m = jnp.bitwise_and(r2, CH - 1)
    c2m = jnp.bitwise_and(c2, CH - 1)
    eye2 = (r2 == c2).astype(F32)
    lane_lo = lax.broadcasted_iota(jnp.int32, (CH, 2 * HD), 1) < HD

    st_ref[...] = s0_ref[0]

    def stack(x):
        return jnp.concatenate([jnp.where(lane_lo, x, 0.0), jnp.where(lane_lo, 0.0, x)], axis=0)

    def mixed_chunk(c):
        start = pl.multiple_of(c * CH, CH)
        zc = za_ref[pl.ds(start, CH), :]
        p0 = pl.multiple_of(jnp.maximum(start - 8, 0), 8)
        n0 = pl.multiple_of(jnp.minimum(start + CH, t_len - 8), 8)
        prow = za_ref[pl.ds(p0, 8), :][7:8, :] * (start > 0).astype(F32)
        nrow = za_ref[pl.ds(n0, 8), :][0:1, :] * (start + CH < t_len).astype(F32)
        prev = jnp.where(row_c == 0, prow, pltpu.roll(zc, 1, 0))
        nxt = jnp.where(row_c == CH - 1, nrow, pltpu.roll(zc, CH - 1, 0))
        return zc + mu * (0.5 * (prev + nxt) - zc), start

    def prep(c, d):
        zm, start = mixed_chunk(c)
        r = zm[:, R_OFF:R_OFF + A_W]
        k = zm[:, K_OFF:K_OFF + A_W]
        v = zm[:, V_OFF:V_OFF + A_W]
        wlo2 = zm[:, WLO_OFF:WLO_OFF + 128]
        alo2 = zm[:, ALO_OFF:ALO_OFF + 128]
        kkr = k * kk_ref[...]
        n2 = _mm_exact_rhs(kkr * kkr, hsum)
        kk = kkr / jnp.maximum(jnp.sqrt(n2), 1e-12)
        wl = w0_ref[d:d + 1, :] + _mmk(_split2(jnp.tanh(wlo2)), _split2(w2_ref[d]))
        ld = -jnp.exp(-_softplus(-wl) - 0.5)
        ag = jax.nn.sigmoid(a0_ref[d:d + 1, :] + _mmk(_split2(alo2), _split2(a2_ref[d])))
        kd = k * (1.0 + (ag - 1.0) * ka_ref[...])
        bv = kk * ag
        cum = _mm_exact_lhs(tri_f if d == 0 else tri_b, ld)
        gam = jnp.exp(cum)
        gam_ex = jnp.exp(cum - ld)
        igam = jnp.exp(-cum)
        ops = (-kk * gam_ex, r * gam, bv * igam, kd * igam, v)
        for j, o in enumerate(ops):
            for p in range(NPAIR):
                prep_ref[d, j, p] = o[:, 128 * p:128 * (p + 1)]
        gtot = gam[CH - 1:CH, :] if d == 0 else gam[0:1, :]
        for p in range(NPAIR):
            gam_ref[d, p] = jnp.broadcast_to(gtot[:, 128 * p:128 * (p + 1)], (8, 128))
        bon_ref[d, pl.ds(start, CH), :] = _mm_exact_rhs(r * kd * rk_ref[...], hsum) * v
        if d == 0:
            glo = zm[:, GLO_OFF:GLO_OFF + 128]
            gate_ref[pl.ds(start, CH), :] = _mmk(_split2(jax.nn.sigmoid(glo)), _split2(g2_ref[...]))

    def solve_all(i):
        n2 = 2 * CH
        ps = range(2 * NPAIR)
        dq = [q // NPAIR for q in ps]
        pq = [q % NPAIR for q in ps]
        start = [pl.multiple_of(i * CH, CH), pl.multiple_of((nc - 1 - i) * CH, CH)]
        order = [r2m - c2m, c2m - r2m]
        a_s = [stack(prep_ref[dq[p], 0, pq[p]]) for p in ps]
        r_s = [stack(prep_ref[dq[p], 1, pq[p]]) for p in ps]
        b_p = [_split2(stack(prep_ref[dq[p], 2, pq[p]])) for p in ps]
        k_p = [_split2(stack(prep_ref[dq[p], 3, pq[p]])) for p in ps]
        v_p = [_split2(stack(prep_ref[dq[p], 4, pq[p]])) for p in ps]
        sc = [_mmk_tb(_split2(jnp.concatenate([a_s[p], r_s[p]], axis=0)),
                      tuple(jnp.concatenate([x, y], axis=0) for x, y in zip(b_p[p], k_p[p]))) for p in ps]
        sab = [jnp.where(order[dq[p]] > 0, sc[p][:n2, :n2], 0.0) for p in ps]
        sak = [jnp.where(order[dq[p]] > 0, sc[p][:n2, n2:], 0.0) for p in ps]
        nrb = [jnp.where(order[dq[p]] >= 0, sc[p][n2:, :n2], 0.0) for p in ps]
        nrk = [jnp.where(order[dq[p]] >= 0, sc[p][n2:, n2:], 0.0) for p in ps]
        pw = []
        for p in ps:
            pw_p = _split2(sab[p])
            pw.append(_mmk(pw_p, pw_p))
        tinv = [eye2 + sab[p] for p in ps]
        n_dbl = int(math.log2(CH))
        for kk in range(1, n_dbl):
            for p in ps:
                pw_p = _split2(pw[p])
                if kk < n_dbl - 1:
                    both = _mmk(pw_p, _split2(jnp.concatenate([pw[p], tinv[p]], axis=1)))
                    pw[p] = both[:, :n2]
                    tinv[p] = tinv[p] + both[:, n2:]
                else:
                    tinv[p] = tinv[p] + _mmk(pw_p, _split2(tinv[p]))
        sakv = [_mmk(_split2(sak[p]), v_p[p]) for p in ps]
        au_p = [_split2(_mmk(_split2(tinv[p]), _split2(jnp.concatenate([a_s[p], sakv[p]], axis=1))))
                for p in ps]
        ry = [_mmk(_split2(nrb[p]), au_p[p]) for p in ps]
        nv = [_mmk(_split2(nrk[p]), v_p[p]) for p in ps]
        gh = [_mmk_ta(au_p[p], b_p[p]) for p in ps]
        vk = [_mmk_ta(v_p[p], k_p[p]) for p in ps]
        s_p = [_split2(st_ref[p]) for p in ps]
        ys = [_mmk_tb(_split2(r_s[p] + ry[p][:, :n2]), s_p[p]) for p in ps]
        for p in ps:
            gt = gam_ref[dq[p], pq[p]][0:1, :]
            g_t = (eye2 + gh[p][:n2, :]) * gt
            h_t = (gh[p][n2:, :] + vk[p]) * gt
            y_all = ys[p] + ry[p][:, n2:] + nv[p]
            y_ref[dq[p], pq[p], pl.ds(start[dq[p]], CH), :] = y_all[:CH, :] + y_all[CH:, :]
            st_ref[p] = _mmk(s_p[p], _split2(g_t)) + h_t

    def chunk_body(i, carry):
        prep(i, 0)
        prep(nc - 1 - i, 1)
        solve_all(i)
        return carry

    lax.fori_loop(0, nc, chunk_body, 0)

    def out_body(c, carry):
        start = pl.multiple_of(c * CH, CH)
        rows = pl.ds(start, CH)
        ysum = jnp.concatenate([y_ref[0, p, rows, :] + y_ref[1, p, rows, :] for p in range(NPAIR)], axis=1)
        mean = _mm_exact_rhs(ysum, hsum) * (1.0 / HD)
        dev = ysum - mean
        var = _mm_exact_rhs(dev * dev, hsum) * (1.0 / HD)
        gn = dev * lax.rsqrt(var + GN_EPS) * lng_ref[...] + lnb_ref[...]
        bonus = bon_ref[0, rows, :] + bon_ref[1, rows, :]
        ya_ref[rows, :] = ((gn + bonus) * gate_ref[rows, :]).astype(ya_ref.dtype)
        return carry

    lax.fori_loop(0, nc, out_body, 0)
    sfin_ref[0] = st_ref[...]


def _rwkv(za, s0bd, t_len, first_token, params):
    n_seq = s0bd.shape[0]
    blk0 = first_token // t_len
    nq = 2 * NPAIR
    full = lambda a: pl.BlockSpec(a.shape, lambda s, _n=a.ndim: (0,) * _n)
    return pl.pallas_call(
        _rwkv_kernel,
        out_shape=(jax.ShapeDtypeStruct((n_seq * t_len, A_W), BF16),
                   jax.ShapeDtypeStruct((n_seq, nq, 128, 128), F32)),
        grid=(n_seq,),
        in_specs=[pl.BlockSpec((t_len, A_COLS), lambda s: (s + blk0, 0)),
                  pl.BlockSpec((1, nq, 128, 128), lambda s: (s, 0, 0, 0))] + [full(a) for a in params],
        out_specs=(pl.BlockSpec((t_len, A_W), lambda s: (s, 0)),
                   pl.BlockSpec((1, nq, 128, 128), lambda s: (s, 0, 0, 0))),
        scratch_shapes=[pltpu.VMEM((2, 5, NPAIR, CH, 128), F32),
                        pltpu.VMEM((2, NPAIR, 8, 128), F32),
                        pltpu.VMEM((nq, 128, 128), F32),
                        pltpu.VMEM((2, NPAIR, t_len, 128), F32),
                        pltpu.VMEM((2, t_len, A_W), F32),
                        pltpu.VMEM((t_len, A_W), F32)],
        compiler_params=_cparams(("arbitrary",)),
        name="rwkv_t%d" % t_len,
    )(za, s0bd, *params)


def _states_to_blockdiag(s):
    n = s.shape[0]
    s = s.reshape(n, 2, NPAIR, 2, HD, HD)
    z = jnp.zeros_like(s[:, :, :, 0])
    top = jnp.concatenate([s[:, :, :, 0], z], axis=-1)
    bot = jnp.concatenate([z, s[:, :, :, 1]], axis=-1)
    return jnp.concatenate([top, bot], axis=-2).reshape(n, 2 * NPAIR, 128, 128)


def _blockdiag_to_states(b):
    n = b.shape[0]
    b = b.reshape(n, 2, NPAIR, 128, 128)
    s = jnp.stack([b[..., :HD, :HD], b[..., HD:, HD:]], axis=3)
    return s.reshape(n, 2, 2 * NPAIR, HD, HD)


def _post_kernel(ncb, xc_ref, xl_ref, ya_ref, ub_ref, mod_ref, lvg_ref, lvb_ref, ws_ref, bs_ref, wo_ref, o_ref):
    tm = xc_ref.shape[0]
    ub = ub_ref[...]
    parts = []
    for g in range(B_W // GD):
        u = ub[:, GD * g:GD * (g + 1)]
        vb = ub[:, B_W + GD * g:B_W + GD * (g + 1)]
        m = jnp.mean(vb, axis=-1, keepdims=True)
        dv = vb - m
        var = jnp.mean(dv * dv, axis=-1, keepdims=True)
        vn = dv * lax.rsqrt(var + LN_EPS) * lvg_ref[:, GD * g:GD * (g + 1)] + lvb_ref[:, GD * g:GD * (g + 1)]
        sp = jnp.concatenate(
            [_mm3(ws_ref[g], vn[GD * c:GD * (c + 1), :]) + bs_ref[g] for c in range(tm // GD)], axis=0)
        parts.append((u * sp).astype(BF16))
    y = jnp.concatenate([ya_ref[...]] + parts, axis=1)
    o_ref[...] = _stream_block(ncb, xc_ref, xl_ref) + _mod_part(mod_ref[0], 2) * _dot(y, wo_ref[...])


def _post(xc, xl, ya, ub, mod, lvg, lvb, ws, bsb, wo, rowmap, tm):
    n = xc.shape[0] + xl.shape[0]
    c2 = lambda i: (0, 0)
    c3 = lambda i: (0, 0, 0)
    return pl.pallas_call(
        functools.partial(_post_kernel, xc.shape[0] // tm),
        out_shape=jax.ShapeDtypeStruct((n, D), F32),
        grid=(n // tm,),
        in_specs=[*_stream_specs(xc.shape[0], tm),
                  pl.BlockSpec((tm, A_W), lambda i: (i, 0)),
                  pl.BlockSpec((tm, 2 * B_W), lambda i: (i, 0)),
                  pl.BlockSpec((1, 1, 6 * D), rowmap),
                  pl.BlockSpec((1, B_W), c2), pl.BlockSpec((1, B_W), c2),
                  pl.BlockSpec(ws.shape, c3), pl.BlockSpec(bsb.shape, c3),
                  pl.BlockSpec((D, D), c2)],
        out_specs=pl.BlockSpec((tm, D), lambda i: (i, 0)),
        compiler_params=_cparams(("arbitrary",)),
        name="gmlp_outproj",
    )(xc, xl, ya, ub, mod, lvg, lvb, ws, bsb, wo)


def _query_kernel(x_ref, g_ref, mod_ref, wq_ref, hq_ref, q_ref):
    h = _norm_mod(x_ref[...], g_ref[...], mod_ref[0], 3, 4).astype(BF16)
    hq_ref[...] = h
    q_ref[...] = _dot(h, wq_ref[...])


def _query(x, g, mod, wq, rowmap, tm):
    n = x.shape[0]
    nq = wq.shape[1]
    return pl.pallas_call(
        _query_kernel,
        out_shape=(jax.ShapeDtypeStruct((n, D), BF16), jax.ShapeDtypeStruct((n, nq), F32)),
        grid=(n // tm,),
        in_specs=[pl.BlockSpec((tm, D), lambda i: (i, 0)),
                  pl.BlockSpec((1, D), lambda i: (0, 0)),
                  pl.BlockSpec((1, 1, 6 * D), rowmap),
                  pl.BlockSpec((D, nq), lambda i: (0, 0))],
        out_specs=(pl.BlockSpec((tm, D), lambda i: (i, 0)), pl.BlockSpec((tm, nq), lambda i: (i, 0))),
        compiler_params=_cparams(("arbitrary",)),
        name="peer_query",
    )(x, g, mod, wq)


def _extract_top(s, rowid, n_rows, count):
    rank = jnp.full(s.shape, float(count), F32)
    vals = []
    for r in range(count):
        m = jnp.max(s, axis=0, keepdims=True)
        idx = jnp.min(jnp.where(s == m, rowid, float(n_rows)), axis=0, keepdims=True)
        hit = rowid == idx
        rank = jnp.where(hit, float(r), rank)
        s = jnp.where(hit, -jnp.inf, s)
        vals.append(m)
    return rank, vals


_CAND_Q = [TOPK // (p + 1) for p in range(TOPK)]


def _pull_distinct(s, count, want_rank=False):
    vals = []
    rank = jnp.full(s.shape, float(count), F32) if want_rank else None
    for r in range(count):
        m = jnp.max(s, axis=0, keepdims=True)
        hit = s == m
        if want_rank:
            rank = jnp.where(hit, float(r), rank)
        s = jnp.where(hit, -jnp.inf, s)
        vals.append(m)
    return (vals, rank) if want_rank else vals


def _candidates(a, b, tb):
    bmat = jnp.concatenate(b, axis=0)
    amat = jnp.concatenate(a, axis=0)
    row8 = lax.broadcasted_iota(jnp.int32, (8, tb), 0)
    blocks = [a[0] + bmat]
    for p in range(1, 8):
        blocks.append(jnp.where(row8 < _CAND_Q[p], a[p] + bmat[:8, :], -jnp.inf))
    blocks.append(amat[8:, :] + b[0])
    return jnp.concatenate(blocks, axis=0)


def _cand_counts(self32):
    cnt = [jnp.sum(self32[0:16], axis=0, keepdims=True)]
    for p in range(1, 8):
        cnt.append(jnp.sum(self32[8 + 8 * p:16 + 8 * p], axis=0, keepdims=True))
    for p in range(8, 16):
        cnt.append(self32[72 + (p - 8):73 + (p - 8)])
    return cnt


def _topk_kernel(q_ref, sk_ref, cnt1_ref, g1_ref, rank2_ref, g2_ref):
    tb = q_ref.shape[0]
    q = q_ref[...]
    s1 = _mm3(sk_ref[0], q[:, :N_KEYS], _dot_tb)
    s2 = _mm3(sk_ref[1], q[:, N_KEYS:], _dot_tb)

    a = _pull_distinct(s1, TOPK)
    b, rank2 = _pull_distinct(s2, TOPK, want_rank=True)
    cand = _candidates(a, b, tb)
    cthr = _pull_distinct(cand, TOPK)[TOPK - 1]
    sel = cand >= cthr
    top = a[0] + b[0]
    z = jnp.sum(jnp.where(sel, jnp.exp(cand - top), 0.0), axis=0, keepdims=True)
    self32 = sel.astype(F32)
    cnt = _cand_counts(self32)
    cnt1 = jnp.zeros((N_KEYS, tb), F32)
    for p in range(TOPK):
        cnt1 = jnp.where(s1 == a[p], cnt[p], cnt1)
    cnt1_ref[0] = cnt1
    g1_ref[0] = jnp.exp(s1 - a[0]) / z
    rank2_ref[0] = rank2.astype(rank2_ref.dtype)
    g2_ref[0] = jnp.exp(s2 - b[0]).astype(g2_ref.dtype)

    n1 = jnp.sum((s1 >= a[TOPK - 1]).astype(F32), axis=0, keepdims=True)
    n2 = jnp.sum((s2 >= b[TOPK - 1]).astype(F32), axis=0, keepdims=True)
    nc = jnp.sum(self32, axis=0, keepdims=True)
    bad = jnp.abs(n1 - TOPK) + jnp.abs(n2 - TOPK) + jnp.abs(nc - TOPK)

    @pl.when(jnp.max(bad) > 0.0)
    def _():
        rowid = lax.broadcasted_iota(jnp.int32, (N_KEYS, tb), 0).astype(F32)
        rank1, ax = _extract_top(s1, rowid, N_KEYS, TOPK)
        rank2x, bx = _extract_top(s2, rowid, N_KEYS, TOPK)
        candx = _candidates(ax, bx, tb)
        n_cand = candx.shape[0]
        crow = lax.broadcasted_iota(jnp.int32, (n_cand, tb), 0).astype(F32)
        crank, _ = _extract_top(candx, crow, n_cand, TOPK)
        selx = crank < float(TOPK)
        zx = jnp.sum(jnp.where(selx, jnp.exp(candx - top), 0.0), axis=0, keepdims=True)
        cntx = _cand_counts(selx.astype(F32))
        cnt1x = jnp.zeros((N_KEYS, tb), F32)
        for p in range(TOPK):
            cnt1x = jnp.where(rank1 == float(p), cntx[p], cnt1x)
        cnt1_ref[0] = cnt1x
        g1_ref[0] = jnp.exp(s1 - a[0]) / zx
        rank2_ref[0] = rank2x.astype(rank2_ref.dtype)


def _topk(q, sk, tb):
    n = q.shape[0]
    shp = jax.ShapeDtypeStruct((PEER_HEADS, N_KEYS, n), F32)
    shp16 = jax.ShapeDtypeStruct((PEER_HEADS, N_KEYS, n), BF16)
    ospec = pl.BlockSpec((1, N_KEYS, tb), lambda i, h: (h, 0, i))
    return pl.pallas_call(
        _topk_kernel,
        out_shape=(shp, shp, shp16, shp16),
        grid=(n // tb, PEER_HEADS),
        in_specs=[pl.BlockSpec((tb, 2 * N_KEYS), lambda i, h: (i, h)),
                  pl.BlockSpec((2, N_KEYS, N_KEYS), lambda i, h: (0, 0, 0))],
        out_specs=(ospec, ospec, ospec, ospec),
        compiler_params=_cparams(("arbitrary", "arbitrary")),
        name="peer_topk",
    )(q, sk)


I1_TILE = 8


GATE_LANES = 256
GATE_BLOCK = (2, 4)
GATE_ROWS = 16
ACT_ROWS = 256


def _experts_kernel(hq_ref, eu_ref, evt_ref, cnt1_ref, g1_ref, rank2_ref, g2_ref,
                    o_ref, acc_ref, w_ref, cb_ref, gb_ref):
    j = pl.program_id(1)
    tm = hq_ref.shape[0]

    @pl.when(j == 0)
    def _():
        acc_ref[...] = jnp.zeros_like(acc_ref)

    for h in range(PEER_HEADS):
        for ii in range(I1_TILE):
            rows = slice(GATE_ROWS * ii, GATE_ROWS * (ii + 1))
            cb_ref[h, rows, :] = jnp.broadcast_to(cnt1_ref[h, ii:ii + 1, :], (GATE_ROWS, tm)).astype(BF16)
            gb_ref[h, rows, :] = jnp.broadcast_to(g1_ref[h, ii:ii + 1, :], (GATE_ROWS, tm)).astype(BF16)

    n_rg = N_KEYS // GATE_ROWS
    na, nb = GATE_BLOCK
    for lc in range(tm // GATE_LANES):
        lanes = slice(GATE_LANES * lc, GATE_LANES * (lc + 1))
        for ii0 in range(0, I1_TILE, na):
            for rg0 in range(0, n_rg, nb):
                accs = [[jnp.zeros((GATE_ROWS, GATE_LANES), BF16) for _ in range(nb)] for _ in range(na)]
                for h in range(PEER_HEADS):
                    rk = [rank2_ref[h, GATE_ROWS * (rg0 + b):GATE_ROWS * (rg0 + b + 1), lanes] for b in range(nb)]
                    g2 = [g2_ref[h, GATE_ROWS * (rg0 + b):GATE_ROWS * (rg0 + b + 1), lanes] for b in range(nb)]
                    for a in range(na):
                        c = cb_ref[h, GATE_ROWS * (ii0 + a):GATE_ROWS * (ii0 + a + 1), lanes]
                        g = gb_ref[h, GATE_ROWS * (ii0 + a):GATE_ROWS * (ii0 + a + 1), lanes]
                        for b in range(nb):
                            accs[a][b] = accs[a][b] + jnp.where(rk[b] < c, g2[b] * g, jnp.zeros_like(g))
                for a in range(na):
                    for b in range(nb):
                        r0 = N_KEYS * (ii0 + a) + GATE_ROWS * (rg0 + b)
                        w_ref[r0:r0 + GATE_ROWS, lanes] = accs[a][b]

    hq = hq_ref[...]
    for rc in range(w_ref.shape[0] // ACT_ROWS):
        rows = slice(ACT_ROWS * rc, ACT_ROWS * (rc + 1))
        act = _dot_tb(eu_ref[rows, :], hq)
        gel = act * (lax.erf(act * (1.0 / math.sqrt(2.0))) + 1.0) * 0.5
        w_ref[rows, :] = w_ref[rows, :] * gel.astype(BF16)
    acc_ref[...] += _dot(evt_ref[...], w_ref[...])

    @pl.when(j == pl.num_programs(1) - 1)
    def _():
        o_ref[...] = acc_ref[...].T


def _experts(hq, eu, evt, layer, cnt1, g1, rank2, g2, tm):
    n = hq.shape[0]
    n_exp = eu.shape[1]
    te = I1_TILE * N_KEYS
    assert tm % GATE_LANES == 0 and te % ACT_ROWS == 0 and n % tm == 0
    return pl.pallas_call(
        _experts_kernel,
        out_shape=jax.ShapeDtypeStruct((n, D), F32),
        grid=(n // tm, n_exp // te),
        in_specs=[pl.BlockSpec((tm, D), lambda i, j: (i, 0)),
                  pl.BlockSpec((None, te, D), lambda i, j: (layer, j, 0)),
                  pl.BlockSpec((None, D, te), lambda i, j: (layer, 0, j)),
                  pl.BlockSpec((PEER_HEADS, I1_TILE, tm), lambda i, j: (0, j, i)),
                  pl.BlockSpec((PEER_HEADS, I1_TILE, tm), lambda i, j: (0, j, i)),
                  pl.BlockSpec((PEER_HEADS, N_KEYS, tm), lambda i, j: (0, 0, i)),
                  pl.BlockSpec((PEER_HEADS, N_KEYS, tm), lambda i, j: (0, 0, i))],
        out_specs=pl.BlockSpec((tm, D), lambda i, j: (i, 0)),
        scratch_shapes=[pltpu.VMEM((D, tm), F32), pltpu.VMEM((te, tm), BF16),
                        pltpu.VMEM((PEER_HEADS, I1_TILE * GATE_ROWS, tm), BF16),
                        pltpu.VMEM((PEER_HEADS, I1_TILE * GATE_ROWS, tm), BF16)],
        compiler_params=_cparams(("arbitrary", "arbitrary")),
        name="peer_experts",
    )(hq, eu, evt, cnt1, g1, rank2, g2)


def _pw1_kernel(x_ref, pe_ref, modp_ref, g_ref, mod_ref, wa_ref, wb_ref, ba_ref, bb_ref, o_ref, x1_ref):
    x1 = x_ref[...] + _mod_part(modp_ref[0], 5) * pe_ref[...]
    x1_ref[...] = x1
    h = _norm_mod(x1, g_ref[...], mod_ref[0], 0, 1).astype(BF16)
    a = _dot(h, wa_ref[...]) + ba_ref[...]
    b = _dot(h, wb_ref[...]) + bb_ref[...]
    o_ref[...] = a * jax.nn.sigmoid(b)


def _pw1(x, pe, mod_prev, g, mod, wa, wb, ba, bb, rowmap, tm):
    n = x.shape[0]
    c2 = lambda i: (0, 0)
    tok = pl.BlockSpec((tm, D), lambda i: (i, 0))
    return pl.pallas_call(
        _pw1_kernel,
        out_shape=(jax.ShapeDtypeStruct((n, D), F32), jax.ShapeDtypeStruct((n, D), F32)),
        grid=(n // tm,),
        in_specs=[tok, tok,
                  pl.BlockSpec((1, 1, 6 * D), rowmap),
                  pl.BlockSpec((1, D), c2),
                  pl.BlockSpec((1, 1, 6 * D), rowmap),
                  pl.BlockSpec((D, D), c2), pl.BlockSpec((D, D), c2),
                  pl.BlockSpec((1, D), c2), pl.BlockSpec((1, D), c2)],
        out_specs=(tok, tok),
        compiler_params=_cparams(("arbitrary",)),
        name="conv_pw1_glu",
    )(x, pe, mod_prev, g, mod, wa, wb, ba, bb)


def _conv_kernel(n_ctx_blocks, seg_ctx, seg_lat,
                 z_ref, x_ref, mod_ref, ck_ref, cb_ref, lg_ref, lb_ref, w2_ref, b2_ref, o_ref,
                 pad_ref, sh_ref, acc_ref):
    tm = z_ref.shape[0]
    i = pl.program_id(0)

    def dwconv(seg):
        nseg = tm // seg
        pitch = seg + CONV_PAD
        zeros = jnp.zeros((CONV_PAD, D), F32)
        for s in range(nseg + 1):
            pad_ref[pitch * s:pitch * s + CONV_PAD, :] = zeros
        for s in range(nseg):
            pad_ref[CONV_PAD + pitch * s:CONV_PAD + pitch * s + seg, :] = z_ref[seg * s:seg * (s + 1), :]
        used = pitch * nseg + CONV_PAD
        span = used - SUBLANES
        for ph in range(SUBLANES):
            sh_ref[ph, 0:span, :] = pad_ref[ph:ph + span, :]
        acc = None
        for k in range(CONV_W):
            row0 = CONV_PAD + k - CONV_HALF
            ph = row0 % SUBLANES
            base = row0 - ph
            parts = [sh_ref[ph, base + pitch * s:base + pitch * s + seg, :] for s in range(nseg)]
            xs = parts[0] if nseg == 1 else jnp.concatenate(parts, axis=0)
            term = xs * ck_ref[k:k + 1, :]
            acc = term if acc is None else acc + term
        acc_ref[...] = acc

    @pl.when(i < n_ctx_blocks)
    def _():
        dwconv(seg_ctx)

    @pl.when(i >= n_ctx_blocks)
    def _():
        dwconv(seg_lat)

    z = acc_ref[...] + cb_ref[...]
    m = jnp.mean(z, axis=-1, keepdims=True)
    dv = z - m
    var = jnp.mean(dv * dv, axis=-1, keepdims=True)
    zn = dv * lax.rsqrt(var + LN_EPS) * lg_ref[...] + lb_ref[...]
    act = (zn * jax.nn.sigmoid(zn)).astype(BF16)
    y = _dot(act, w2_ref[...]) + b2_ref[...]
    o_ref[...] = x_ref[...] + _mod_part(mod_ref[0], 2) * y


def _conv(z, x, mod, ck, cb, lg, lb, w2, b2, rowmap, tm, n_ctx_blocks, seg_ctx, seg_lat):
    n = x.shape[0]
    assert tm % seg_ctx == 0 and tm % seg_lat == 0 and CONV_PAD >= CONV_HALF and CONV_PAD % SUBLANES == 0
    pad_rows = tm + (tm // min(seg_ctx, seg_lat) + 1) * CONV_PAD
    c2 = lambda i: (0, 0)
    return pl.pallas_call(
        functools.partial(_conv_kernel, n_ctx_blocks, seg_ctx, seg_lat),
        out_shape=jax.ShapeDtypeStruct((n, D), F32),
        grid=(n // tm,),
        in_specs=[pl.BlockSpec((tm, D), lambda i: (i, 0)),
                  pl.BlockSpec((tm, D), lambda i: (i, 0)),
                  pl.BlockSpec((1, 1, 6 * D), rowmap),
                  pl.BlockSpec(ck.shape, c2),
                  pl.BlockSpec((1, D), c2), pl.BlockSpec((1, D), c2), pl.BlockSpec((1, D), c2),
                  pl.BlockSpec((D, D), c2), pl.BlockSpec((1, D), c2)],
        out_specs=pl.BlockSpec((tm, D), lambda i: (i, 0)),
        scratch_shapes=[pltpu.VMEM((pad_rows, D), F32),
                        pltpu.VMEM((SUBLANES, pad_rows, D), F32),
                        pltpu.VMEM((tm, D), F32)],
        compiler_params=_cparams(("arbitrary",)),
        name="conv_dw_ln_pw2",
    )(z, x, mod, ck, cb, lg, lb, w2, b2)


def _final_kernel(ncb, x_ref, pe_ref, mod_ref, g_ref, oc_ref, ol_ref):
    x = x_ref[...] + _mod_part(mod_ref[0], 5) * pe_ref[...]
    ms = jnp.mean(x * x, axis=-1, keepdims=True)
    y = x * lax.rsqrt(ms + EPS) * g_ref[...]
    i = pl.program_id(0)

    @pl.when(i < ncb)
    def _():
        oc_ref[...] = y

    @pl.when(i >= ncb)
    def _():
        ol_ref[...] = y


def _final_norm(x, pe, mod, g, rowmap, tm, n_ctx_tokens):
    n = x.shape[0]
    tok = pl.BlockSpec((tm, D), lambda i: (i, 0))
    return pl.pallas_call(
        functools.partial(_final_kernel, n_ctx_tokens // tm),
        out_shape=(jax.ShapeDtypeStruct((n_ctx_tokens, D), F32),
                   jax.ShapeDtypeStruct((n - n_ctx_tokens, D), F32)),
        grid=(n // tm,),
        in_specs=[tok, tok, pl.BlockSpec((1, 1, 6 * D), rowmap), pl.BlockSpec((1, D), lambda i: (0, 0))],
        out_specs=_stream_specs(n_ctx_tokens, tm),
        compiler_params=_cparams(("arbitrary",)),
        name="final_norm",
    )(x, pe, mod, g)


def _peer_layer(x, l, mod_l, norm2, w_query, sub_keys, eu_all, evt_all, rowmap_of):
    hq, q = _query(x, norm2[l][None, :], mod_l, w_query[l].astype(BF16), rowmap_of(512), 512)
    cnt1, g1, rank2, g2 = _topk(q, sub_keys[l], 512)
    return _experts(hq, eu_all, evt_all, l, cnt1, g1, rank2, g2, 1024)


def kernel(x_prompt, x_sample, state_rwkv, c, c_ctx, w_mod, b_mod, norm1, norm2, norm_f, w_in_ab, mu_a, w0, w2, a0, a2, g2, k_k, k_a, r_k, lnx_g, lnx_b, lnv_g, lnv_b, w_s, b_s, w_out_ab, w_pw1, b_pw1, conv_k, conv_b, lnc_g, lnc_b, w_pw2, b_pw2, w_query, sub_keys, expert_u, expert_v):
    n_ctx, ctx_len, _ = x_prompt.shape
    n_lat, lat_len, _ = x_sample.shape
    nc_tok = n_ctx * ctx_len
    grid_w = 64
    xc = x_prompt.reshape(nc_tok, D)
    xl = x_sample.reshape(n_lat * lat_len, D)
    eu_all = expert_u.astype(BF16)
    evt_all = jnp.swapaxes(expert_v.astype(BF16), 1, 2)

    cvec8 = jnp.zeros((8, D), F32).at[0].set(c_ctx).at[1:1 + n_lat].set(c)
    mod = _modulation(cvec8, w_mod, b_mod)
    mods = [mod[l].reshape(8, 1, 6 * D) for l in range(mod.shape[0])]
    rowmap_of = lambda tm: _mod_row_map(nc_tok, lat_len, tm)

    w_in = w_in_ab[0]
    za, ub = _inproj(xc, xl, norm1[0][None, :], mods[0], w_in[:, :A_COLS].astype(BF16),
                     w_in[:, A_COLS:].astype(BF16), rowmap_of(512), 512)
    row = lambda a: a.reshape(1, -1)
    pad_lora = lambda w, off: jnp.zeros((2, 128, A_W), F32).at[0, :HD].set(w[0]).at[1, HD:].set(w[1])
    heads = jnp.arange(A_W) // HD
    hsum = (heads[:, None] == heads[None, :]).astype(BF16)
    rparams = (row(mu_a[0]), w0[0], pad_lora(w2[0], 0), a0[0], pad_lora(a2[0], 0), g2[0],
               row(k_k[0]), row(k_a[0]), row(r_k[0]), row(lnx_g[0]), row(lnx_b[0]), hsum)
    zero_state = jnp.zeros((n_ctx, 2 * NPAIR, 128, 128), F32)
    ya_c, sfin = _rwkv(za, zero_state, ctx_len, 0, rparams)
    ya_s, _ = _rwkv(za, _states_to_blockdiag(state_rwkv[:, 0]), lat_len, nc_tok, rparams)
    ya = jnp.concatenate([ya_c, ya_s], axis=0)
    bsb = jnp.broadcast_to(b_s[0][:, :, None], (b_s.shape[1], GD, GD))
    x = _post(xc, xl, ya, ub, mods[0], row(lnv_g[0]), row(lnv_b[0]), w_s[0], bsb,
              w_out_ab[0].astype(BF16), rowmap_of(512), 512)
    pe = _peer_layer(x, 0, mods[0], norm2, w_query, sub_keys, eu_all, evt_all, rowmap_of)

    glu, x = _pw1(x, pe, mods[0], norm1[1][None, :], mods[1], w_pw1[0][:, :D].astype(BF16),
                  w_pw1[0][:, D:].astype(BF16), row(b_pw1[0][:D]), row(b_pw1[0][D:]), rowmap_of(512), 512)
    ck = jnp.zeros((32, D), F32).at[:CONV_W].set(conv_k[0])
    x = _conv(glu, x, mods[1], ck, row(conv_b[0]), row(lnc_g[0]), row(lnc_b[0]),
              w_pw2[0].astype(BF16), row(b_pw2[0]), rowmap_of(256), 256,
              nc_tok // 256, ctx_len, grid_w)
    pe = _peer_layer(x, 1, mods[1], norm2, w_query, sub_keys, eu_all, evt_all, rowmap_of)

    yc, yl = _final_norm(x, pe, mods[1], norm_f[None, :], rowmap_of(512), 512, nc_tok)
    y_prompt = yc.reshape(n_ctx, ctx_len, D)
    y_sample = yl.reshape(n_lat, lat_len, D)
    new_state = _blockdiag_to_states(sfin)[:, None].astype(x_prompt.dtype)
    return (y_prompt, y_sample, new_state)
```

```python
import functools
import math

import jax
import jax.numpy as jnp
from jax import lax
from jax.experimental import pallas as pl
from jax.experimental.pallas import tpu as pltpu

F32 = jnp.float32
BF16 = jnp.bfloat16

D = 1024
HD = 64
CH = 64
A_W = 512
NPAIR = A_W // 128
R_OFF, K_OFF, V_OFF = 0, 512, 1024
WLO_OFF, ALO_OFF, GLO_OFF, A_COLS = 1536, 1664, 1792, 1920
B_W = 512
GD = 128
CONV_W = 31
CONV_HALF = 15
CONV_PAD = 16
SUBLANES = 8
N_KEYS = 128
TOPK = 16
PEER_HEADS = 8
EPS = 1e-6
LN_EPS = 1e-5
GN_EPS = 64e-5
VMEM_LIMIT = 56 * 1024 * 1024


def _cparams(sem):
    return pltpu.CompilerParams(dimension_semantics=sem, vmem_limit_bytes=VMEM_LIMIT)


def _dot(a, b):
    return jnp.dot(a, b, preferred_element_type=F32)


def _dot_tb(a, b):
    return lax.dot_general(a, b, (((1,), (1,)), ((), ())), preferred_element_type=F32)


def _dot_ta(a, b):
    return lax.dot_general(a, b, (((0,), (0,)), ((), ())), preferred_element_type=F32)


def _split2(x):
    hi = x.astype(BF16)
    lo = (x - hi.astype(F32)).astype(BF16)
    return hi, lo


def _split3(x):
    hi = x.astype(BF16)
    r1 = x - hi.astype(F32)
    mid = r1.astype(BF16)
    lo = (r1 - mid.astype(F32)).astype(BF16)
    return hi, mid, lo


def _mm3(a, b, dot=_dot):
    ah, al = _split2(a)
    bh, bl = _split2(b)
    return dot(ah, bh) + dot(ah, bl) + dot(al, bh)


def _bf(x):
    return x.astype(BF16)


def _mm_exact_lhs(m_bf16, x):
    hi, mid, lo = _split3(x)
    return _dot(m_bf16, hi) + _dot(m_bf16, mid) + _dot(m_bf16, lo)


def _mm_exact_rhs(x, m_bf16):
    hi, lo = _split2(x)
    return _dot(hi, m_bf16) + _dot(lo, m_bf16)


def _mod_part(mod_row, j):
    return mod_row[:, j * D:(j + 1) * D]


def _norm_mod(x, g, mod_row, j_shift, j_scale):
    ms = jnp.mean(x * x, axis=-1, keepdims=True)
    y = x * lax.rsqrt(ms + EPS) * g
    return y * (1.0 + _mod_part(mod_row, j_scale)) + _mod_part(mod_row, j_shift)


def _mod_row_map(n_ctx_tokens, seq_tokens, tm):
    ncb = n_ctx_tokens // tm
    per = seq_tokens // tm

    def imap(i, *_):
        return (jnp.where(i < ncb, 0, 1 + (i - ncb) // per), 0, 0)
    return imap


def _mod_kernel(c_ref, w_ref, b_ref, o_ref):
    c = c_ref[...]
    s = c * jax.nn.sigmoid(c)
    o_ref[0] = _mm3(s, w_ref[0]) + b_ref[0]


def _modulation(cvec8, w_mod, b_mod):
    depth, _, n6 = w_mod.shape
    tn = 1536
    return pl.pallas_call(
        _mod_kernel,
        out_shape=jax.ShapeDtypeStruct((depth, 8, n6), F32),
        grid=(depth, n6 // tn),
        in_specs=[pl.BlockSpec((8, D), lambda l, j: (0, 0)),
                  pl.BlockSpec((1, D, tn), lambda l, j: (l, 0, j)),
                  pl.BlockSpec((1, 1, tn), lambda l, j: (l, 0, j))],
        out_specs=pl.BlockSpec((1, 8, tn), lambda l, j: (l, 0, j)),
        compiler_params=_cparams(("arbitrary", "arbitrary")),
        name="modulation",
    )(cvec8, w_mod, b_mod.reshape(depth, 1, n6))


def _stream_specs(n_ctx_tokens, tm):
    ncb = n_ctx_tokens // tm
    return (pl.BlockSpec((tm, D), lambda i: (jnp.minimum(i, ncb - 1), 0)),
            pl.BlockSpec((tm, D), lambda i: (jnp.maximum(i - ncb, 0), 0)))


def _stream_block(ncb, xc_ref, xl_ref):
    return jnp.where(pl.program_id(0) < ncb, xc_ref[...], xl_ref[...])


def _inproj_kernel(ncb, xc_ref, xl_ref, g_ref, mod_ref, wa_ref, wb_ref, za_ref, ub_ref):
    x = _stream_block(ncb, xc_ref, xl_ref)
    h = _norm_mod(x, g_ref[...], mod_ref[0], 0, 1).astype(BF16)
    za_ref[...] = _dot(h, wa_ref[...])
    ub_ref[...] = _dot(h, wb_ref[...])


def _inproj(xc, xl, g, mod, wa, wb, rowmap, tm):
    n = xc.shape[0] + xl.shape[0]
    return pl.pallas_call(
        functools.partial(_inproj_kernel, xc.shape[0] // tm),
        out_shape=(jax.ShapeDtypeStruct((n, A_COLS), F32), jax.ShapeDtypeStruct((n, 2 * B_W), F32)),
        grid=(n // tm,),
        in_specs=[*_stream_specs(xc.shape[0], tm),
                  pl.BlockSpec((1, D), lambda i: (0, 0)),
                  pl.BlockSpec((1, 1, 6 * D), rowmap),
                  pl.BlockSpec((D, A_COLS), lambda i: (0, 0)),
                  pl.BlockSpec((D, 2 * B_W), lambda i: (0, 0))],
        out_specs=(pl.BlockSpec((tm, A_COLS), lambda i: (i, 0)),
                   pl.BlockSpec((tm, 2 * B_W), lambda i: (i, 0))),
        compiler_params=_cparams(("arbitrary",)),
        name="inproj",
    )(xc, xl, g, mod, wa, wb)


def _softplus(x):
    return jnp.maximum(x, 0.0) + jnp.log1p(jnp.exp(-jnp.abs(x)))


def _rwkv_kernel(za_ref, s0_ref, mu_ref, w0_ref, w2_ref, a0_ref, a2_ref, g2_ref, kk_ref, ka_ref,
                 rk_ref, lng_ref, lnb_ref, hsum_ref, ya_ref, sfin_ref,
                 prep_ref, gam_ref, st_ref, y_ref, bon_ref, gate_ref):
    t_len = za_ref.shape[0]
    nc = t_len // CH
    mu = mu_ref[...]
    hsum = hsum_ref[...]
    row_c = lax.broadcasted_iota(jnp.int32, (CH, A_COLS), 0)
    ri = lax.broadcasted_iota(jnp.int32, (CH, CH), 0)
    ci = lax.broadcasted_iota(jnp.int32, (CH, CH), 1)
    tri_f = (ci <= ri).astype(BF16)
    tri_b = (ci >= ri).astype(BF16)
    r2 = lax.broadcasted_iota(jnp.int32, (2 * CH, 2 * CH), 0)
    c2 = lax.broadcasted_iota(jnp.int32, (2 * CH, 2 * CH), 1)
    r2m = jnp.bitwise_and(r2, CH - 1)
    c2m = jnp.bitwise_and(c2, CH - 1)
    eye2 = (r2 == c2).astype(F32)
    lane_lo = lax.broadcasted_iota(jnp.int32, (CH, 2 * HD), 1) < HD

    st_ref[...] = s0_ref[0]

    def stack(x):
        return jnp.concatenate([jnp.where(lane_lo, x, 0.0), jnp.where(lane_lo, 0.0, x)], axis=0)

    def mixed_chunk(c):
        start = pl.multiple_of(c * CH, CH)
        zc = za_ref[pl.ds(start, CH), :]
        p0 = pl.multiple_of(jnp.maximum(start - 8, 0), 8)
        n0 = pl.multiple_of(jnp.minimum(start + CH, t_len - 8), 8)
        prow = za_ref[pl.ds(p0, 8), :][7:8, :] * (start > 0).astype(F32)
        nrow = za_ref[pl.ds(n0, 8), :][0:1, :] * (start + CH < t_len).astype(F32)
        prev = jnp.where(row_c == 0, prow, pltpu.roll(zc, 1, 0))
        nxt = jnp.where(row_c == CH - 1, nrow, pltpu.roll(zc, CH - 1, 0))
        return zc + mu * (0.5 * (prev + nxt) - zc), start

    def prep(c, d):
        zm, start = mixed_chunk(c)
        r = zm[:, R_OFF:R_OFF + A_W]
        k = zm[:, K_OFF:K_OFF + A_W]
        v = zm[:, V_OFF:V_OFF + A_W]
        wlo2 = zm[:, WLO_OFF:WLO_OFF + 128]
        alo2 = zm[:, ALO_OFF:ALO_OFF + 128]
        kkr = k * kk_ref[...]
        n2 = _mm_exact_rhs(kkr * kkr, hsum)
        kk = kkr / jnp.maximum(jnp.sqrt(n2), 1e-12)
        wl = w0_ref[d:d + 1, :] + _dot(_bf(jnp.tanh(wlo2)), _bf(w2_ref[d]))
        ld = -jnp.exp(-_softplus(-wl) - 0.5)
        ag = jax.nn.sigmoid(a0_ref[d:d + 1, :] + _dot(_bf(alo2), _bf(a2_ref[d])))
        kd = k * (1.0 + (ag - 1.0) * ka_ref[...])
        bv = kk * ag
        cum = _mm_exact_lhs(tri_f if d == 0 else tri_b, ld)
        gam = jnp.exp(cum)
        gam_ex = jnp.exp(cum - ld)
        igam = jnp.exp(-cum)
        ops = (-kk * gam_ex, r * gam, bv * igam, kd * igam, v)
        for j, o in enumerate(ops):
            for p in range(NPAIR):
                prep_ref[d, j, p] = o[:, 128 * p:128 * (p + 1)]
        gtot = gam[CH - 1:CH, :] if d == 0 else gam[0:1, :]
        for p in range(NPAIR):
            gam_ref[d, p] = jnp.broadcast_to(gtot[:, 128 * p:128 * (p + 1)], (8, 128))
        bon_ref[d, pl.ds(start, CH), :] = _mm_exact_rhs(r * kd * rk_ref[...], hsum) * v
        if d == 0:
            glo = zm[:, GLO_OFF:GLO_OFF + 128]
            gate_ref[pl.ds(start, CH), :] = _dot(_bf(jax.nn.sigmoid(glo)), _bf(g2_ref[...]))

    def solve_all(i):
        n2 = 2 * CH
        ps = range(2 * NPAIR)
        dq = [q // NPAIR for q in ps]
        pq = [q % NPAIR for q in ps]
        start = [pl.multiple_of(i * CH, CH), pl.multiple_of((nc - 1 - i) * CH, CH)]
        order = [r2m - c2m, c2m - r2m]
        a_s = [stack(prep_ref[dq[p], 0, pq[p]]) for p in ps]
        r_s = [stack(prep_ref[dq[p], 1, pq[p]]) for p in ps]
        b_h = [_bf(stack(prep_ref[dq[p], 2, pq[p]])) for p in ps]
        k_h = [_bf(stack(prep_ref[dq[p], 3, pq[p]])) for p in ps]
        v_h = [_bf(stack(prep_ref[dq[p], 4, pq[p]])) for p in ps]
        sc = [_dot_tb(_bf(jnp.concatenate([a_s[p], r_s[p]], axis=0)),
                      jnp.concatenate([b_h[p], k_h[p]], axis=0)) for p in ps]
        sab = [jnp.where(order[dq[p]] > 0, sc[p][:n2, :n2], 0.0) for p in ps]
        sak = [jnp.where(order[dq[p]] > 0, sc[p][:n2, n2:], 0.0) for p in ps]
        nrb = [jnp.where(order[dq[p]] >= 0, sc[p][n2:, :n2], 0.0) for p in ps]
        nrk = [jnp.where(order[dq[p]] >= 0, sc[p][n2:, n2:], 0.0) for p in ps]
        pw = []
        for p in ps:
            pw_h = _bf(sab[p])
            pw.append(_dot(pw_h, pw_h))
        tinv = [eye2 + sab[p] for p in ps]
        n_dbl = int(math.log2(CH))
        for kk in range(1, n_dbl):
            for p in ps:
                pw_h = _bf(pw[p])
                if kk < n_dbl - 1:
                    both = _dot(pw_h, jnp.concatenate([pw_h, _bf(tinv[p])], axis=1))
                    pw[p] = both[:, :n2]
                    tinv[p] = tinv[p] + both[:, n2:]
                else:
                    tinv[p] = tinv[p] + _dot(pw_h, _bf(tinv[p]))
        sakv = [_dot(_bf(sak[p]), v_h[p]) for p in ps]
        au_h = [_bf(_dot(_bf(tinv[p]), _bf(jnp.concatenate([a_s[p], sakv[p]], axis=1))))
                for p in ps]
        ry = [_dot(_bf(nrb[p]), au_h[p]) for p in ps]
        nv = [_dot(_bf(nrk[p]), v_h[p]) for p in ps]
        gh = [_dot_ta(au_h[p], b_h[p]) for p in ps]
        vk = [_dot_ta(v_h[p], k_h[p]) for p in ps]
        s_h = [_bf(st_ref[p]) for p in ps]
        ys = [_dot_tb(_bf(r_s[p] + ry[p][:, :n2]), s_h[p]) for p in ps]
        for p in ps:
            gt = gam_ref[dq[p], pq[p]][0:1, :]
            g_t = (eye2 + gh[p][:n2, :]) * gt
            h_t = (gh[p][n2:, :] + vk[p]) * gt
            y_all = ys[p] + ry[p][:, n2:] + nv[p]
            y_ref[dq[p], pq[p], pl.ds(start[dq[p]], CH), :] = y_all[:CH, :] + y_all[CH:, :]
            st_ref[p] = _dot(s_h[p], _bf(g_t)) + h_t

    def chunk_body(i, carry):
        prep(i, 0)
        prep(nc - 1 - i, 1)
        solve_all(i)
        return carry

    lax.fori_loop(0, nc, chunk_body, 0)

    def out_body(c, carry):
        start = pl.multiple_of(c * CH, CH)
        rows = pl.ds(start, CH)
        ysum = jnp.concatenate([y_ref[0, p, rows, :] + y_ref[1, p, rows, :] for p in range(NPAIR)], axis=1)
        mean = _mm_exact_rhs(ysum, hsum) * (1.0 / HD)
        dev = ysum - mean
        var = _mm_exact_rhs(dev * dev, hsum) * (1.0 / HD)
        gn = dev * lax.rsqrt(var + GN_EPS) * lng_ref[...] + lnb_ref[...]
        bonus = bon_ref[0, rows, :] + bon_ref[1, rows, :]
        ya_ref[rows, :] = ((gn + bonus) * gate_ref[rows, :]).astype(ya_ref.dtype)
        return carry

    lax.fori_loop(0, nc, out_body, 0)
    sfin_ref[0] = st_ref[...]


def _rwkv(za, s0bd, t_len, first_token, params):
    n_seq = s0bd.shape[0]
    blk0 = first_token // t_len
    nq = 2 * NPAIR
    full = lambda a: pl.BlockSpec(a.shape, lambda s, _n=a.ndim: (0,) * _n)
    return pl.pallas_call(
        _rwkv_kernel,
        out_shape=(jax.ShapeDtypeStruct((n_seq * t_len, A_W), BF16),
                   jax.ShapeDtypeStruct((n_seq, nq, 128, 128), F32)),
        grid=(n_seq,),
        in_specs=[pl.BlockSpec((t_len, A_COLS), lambda s: (s + blk0, 0)),
                  pl.BlockSpec((1, nq, 128, 128), lambda s: (s, 0, 0, 0))] + [full(a) for a in params],
        out_specs=(pl.BlockSpec((t_len, A_W), lambda s: (s, 0)),
                   pl.BlockSpec((1, nq, 128, 128), lambda s: (s, 0, 0, 0))),
        scratch_shapes=[pltpu.VMEM((2, 5, NPAIR, CH, 128), F32),
                        pltpu.VMEM((2, NPAIR, 8, 128), F32),
                        pltpu.VMEM((nq, 128, 128), F32),
                        pltpu.VMEM((2, NPAIR, t_len, 128), F32),
                        pltpu.VMEM((2, t_len, A_W), F32),
                        pltpu.VMEM((t_len, A_W), F32)],
        compiler_params=_cparams(("arbitrary",)),
        name="rwkv_t%d" % t_len,
    )(za, s0bd, *params)


def _states_to_blockdiag(s):
    n = s.shape[0]
    s = s.reshape(n, 2, NPAIR, 2, HD, HD)
    z = jnp.zeros_like(s[:, :, :, 0])
    top = jnp.concatenate([s[:, :, :, 0], z], axis=-1)
    bot = jnp.concatenate([z, s[:, :, :, 1]], axis=-1)
    return jnp.concatenate([top, bot], axis=-2).reshape(n, 2 * NPAIR, 128, 128)


def _blockdiag_to_states(b):
    n = b.shape[0]
    b = b.reshape(n, 2, NPAIR, 128, 128)
    s = jnp.stack([b[..., :HD, :HD], b[..., HD:, HD:]], axis=3)
    return s.reshape(n, 2, 2 * NPAIR, HD, HD)


def _post_kernel(ncb, xc_ref, xl_ref, ya_ref, ub_ref, mod_ref, lvg_ref, lvb_ref, ws_ref, bs_ref, wo_ref, o_ref):
    tm = xc_ref.shape[0]
    ub = ub_ref[...]
    parts = []
    for g in range(B_W // GD):
        u = ub[:, GD * g:GD * (g + 1)]
        vb = ub[:, B_W + GD * g:B_W + GD * (g + 1)]
        m = jnp.mean(vb, axis=-1, keepdims=True)
        dv = vb - m
        var = jnp.mean(dv * dv, axis=-1, keepdims=True)
        vn = dv * lax.rsqrt(var + LN_EPS) * lvg_ref[:, GD * g:GD * (g + 1)] + lvb_ref[:, GD * g:GD * (g + 1)]
        sp = jnp.concatenate(
            [_mm3(ws_ref[g], vn[GD * c:GD * (c + 1), :]) + bs_ref[g] for c in range(tm // GD)], axis=0)
        parts.append((u * sp).astype(BF16))
    y = jnp.concatenate([ya_ref[...]] + parts, axis=1)
    o_ref[...] = _stream_block(ncb, xc_ref, xl_ref) + _mod_part(mod_ref[0], 2) * _dot(y, wo_ref[...])


def _post(xc, xl, ya, ub, mod, lvg, lvb, ws, bsb, wo, rowmap, tm):
    n = xc.shape[0] + xl.shape[0]
    c2 = lambda i: (0, 0)
    c3 = lambda i: (0, 0, 0)
    return pl.pallas_call(
        functools.partial(_post_kernel, xc.shape[0] // tm),
        out_shape=jax.ShapeDtypeStruct((n, D), F32),
        grid=(n // tm,),
        in_specs=[*_stream_specs(xc.shape[0], tm),
                  pl.BlockSpec((tm, A_W), lambda i: (i, 0)),
                  pl.BlockSpec((tm, 2 * B_W), lambda i: (i, 0)),
                  pl.BlockSpec((1, 1, 6 * D), rowmap),
                  pl.BlockSpec((1, B_W), c2), pl.BlockSpec((1, B_W), c2),
                  pl.BlockSpec(ws.shape, c3), pl.BlockSpec(bsb.shape, c3),
                  pl.BlockSpec((D, D), c2)],
        out_specs=pl.BlockSpec((tm, D), lambda i: (i, 0)),
        compiler_params=_cparams(("arbitrary",)),
        name="gmlp_outproj",
    )(xc, xl, ya, ub, mod, lvg, lvb, ws, bsb, wo)


def _query_kernel(x_ref, g_ref, mod_ref, wq_ref, hq_ref, q_ref):
    h = _norm_mod(x_ref[...], g_ref[...], mod_ref[0], 3, 4).astype(BF16)
    hq_ref[...] = h
    q_ref[...] = _dot(h, wq_ref[...])


def _query(x, g, mod, wq, rowmap, tm):
    n = x.shape[0]
    nq = wq.shape[1]
    return pl.pallas_call(
        _query_kernel,
        out_shape=(jax.ShapeDtypeStruct((n, D), BF16), jax.ShapeDtypeStruct((n, nq), F32)),
        grid=(n // tm,),
        in_specs=[pl.BlockSpec((tm, D), lambda i: (i, 0)),
                  pl.BlockSpec((1, D), lambda i: (0, 0)),
                  pl.BlockSpec((1, 1, 6 * D), rowmap),
                  pl.BlockSpec((D, nq), lambda i: (0, 0))],
        out_specs=(pl.BlockSpec((tm, D), lambda i: (i, 0)), pl.BlockSpec((tm, nq), lambda i: (i, 0))),
        compiler_params=_cparams(("arbitrary",)),
        name="peer_query",
    )(x, g, mod, wq)


def _extract_top(s, rowid, n_rows, count):
    rank = jnp.full(s.shape, float(count), F32)
    vals = []
    for r in range(count):
        m = jnp.max(s, axis=0, keepdims=True)
        idx = jnp.min(jnp.where(s == m, rowid, float(n_rows)), axis=0, keepdims=True)
        hit = rowid == idx
        rank = jnp.where(hit, float(r), rank)
        s = jnp.where(hit, -jnp.inf, s)
        vals.append(m)
    return rank, vals


_CAND_Q = [TOPK // (p + 1) for p in range(TOPK)]


def _pull_distinct(s, count, want_rank=False):
    vals = []
    rank = jnp.full(s.shape, float(count), F32) if want_rank else None
    for r in range(count):
        m = jnp.max(s, axis=0, keepdims=True)
        hit = s == m
        if want_rank:
            rank = jnp.where(hit, float(r), rank)
        s = jnp.where(hit, -jnp.inf, s)
        vals.append(m)
    return (vals, rank) if want_rank else vals


def _candidates(a, b, tb):
    bmat = jnp.concatenate(b, axis=0)
    amat = jnp.concatenate(a, axis=0)
    row8 = lax.broadcasted_iota(jnp.int32, (8, tb), 0)
    blocks = [a[0] + bmat]
    for p in range(1, 8):
        blocks.append(jnp.where(row8 < _CAND_Q[p], a[p] + bmat[:8, :], -jnp.inf))
    blocks.append(amat[8:, :] + b[0])
    return jnp.concatenate(blocks, axis=0)


def _cand_counts(self32):
    cnt = [jnp.sum(self32[0:16], axis=0, keepdims=True)]
    for p in range(1, 8):
        cnt.append(jnp.sum(self32[8 + 8 * p:16 + 8 * p], axis=0, keepdims=True))
    for p in range(8, 16):
        cnt.append(self32[72 + (p - 8):73 + (p - 8)])
    return cnt


def _topk_kernel(q_ref, sk_ref, cnt1_ref, g1_ref, rank2_ref, g2_ref):
    tb = q_ref.shape[0]
    q = q_ref[...]
    s1 = _mm3(sk_ref[0], q[:, :N_KEYS], _dot_tb)
    s2 = _mm3(sk_ref[1], q[:, N_KEYS:], _dot_tb)

    a = _pull_distinct(s1, TOPK)
    b, rank2 = _pull_distinct(s2, TOPK, want_rank=True)
    cand = _candidates(a, b, tb)
    cthr = _pull_distinct(cand, TOPK)[TOPK - 1]
    sel = cand >= cthr
    top = a[0] + b[0]
    z = jnp.sum(jnp.where(sel, jnp.exp(cand - top), 0.0), axis=0, keepdims=True)
    self32 = sel.astype(F32)
    cnt = _cand_counts(self32)
    cnt1 = jnp.zeros((N_KEYS, tb), F32)
    for p in range(TOPK):
        cnt1 = jnp.where(s1 == a[p], cnt[p], cnt1)
    cnt1_ref[0] = cnt1
    g1_ref[0] = jnp.exp(s1 - a[0]) / z
    rank2_ref[0] = rank2.astype(rank2_ref.dtype)
    g2_ref[0] = jnp.exp(s2 - b[0]).astype(g2_ref.dtype)

    n1 = jnp.sum((s1 >= a[TOPK - 1]).astype(F32), axis=0, keepdims=True)
    n2 = jnp.sum((s2 >= b[TOPK - 1]).astype(F32), axis=0, keepdims=True)
    nc = jnp.sum(self32, axis=0, keepdims=True)
    bad = jnp.abs(n1 - TOPK) + jnp.abs(n2 - TOPK) + jnp.abs(nc - TOPK)

    @pl.when(jnp.max(bad) > 0.0)
    def _():
        rowid = lax.broadcasted_iota(jnp.int32, (N_KEYS, tb), 0).astype(F32)
        rank1, ax = _extract_top(s1, rowid, N_KEYS, TOPK)
        rank2x, bx = _extract_top(s2, rowid, N_KEYS, TOPK)
        candx = _candidates(ax, bx, tb)
        n_cand = candx.shape[0]
        crow = lax.broadcasted_iota(jnp.int32, (n_cand, tb), 0).astype(F32)
        crank, _ = _extract_top(candx, crow, n_cand, TOPK)
        selx = crank < float(TOPK)
        zx = jnp.sum(jnp.where(selx, jnp.exp(candx - top), 0.0), axis=0, keepdims=True)
        cntx = _cand_counts(selx.astype(F32))
        cnt1x = jnp.zeros((N_KEYS, tb), F32)
        for p in range(TOPK):
            cnt1x = jnp.where(rank1 == float(p), cntx[p], cnt1x)
        cnt1_ref[0] = cnt1x
        g1_ref[0] = jnp.exp(s1 - a[0]) / zx
        rank2_ref[0] = rank2x.astype(rank2_ref.dtype)


def _topk(q, sk, tb):
    n = q.shape[0]
    shp = jax.ShapeDtypeStruct((PEER_HEADS, N_KEYS, n), F32)
    shp16 = jax.ShapeDtypeStruct((PEER_HEADS, N_KEYS, n), BF16)
    ospec = pl.BlockSpec((1, N_KEYS, tb), lambda i, h: (h, 0, i))
    return pl.pallas_call(
        _topk_kernel,
        out_shape=(shp, shp, shp16, shp16),
        grid=(n // tb, PEER_HEADS),
        in_specs=[pl.BlockSpec((tb, 2 * N_KEYS), lambda i, h: (i, h)),
                  pl.BlockSpec((2, N_KEYS, N_KEYS), lambda i, h: (0, 0, 0))],
        out_specs=(ospec, ospec, ospec, ospec),
        compiler_params=_cparams(("arbitrary", "arbitrary")),
        name="peer_topk",
    )(q, sk)


I1_TILE = 8


GATE_LANES = 256
GATE_BLOCK = (2, 4)
GATE_ROWS = 16
ACT_ROWS = 256


def _experts_kernel(hq_ref, eu_ref, evt_ref, cnt1_ref, g1_ref, rank2_ref, g2_ref,
                    o_ref, acc_ref, w_ref, cb_ref, gb_ref):
    j = pl.program_id(1)
    tm = hq_ref.shape[0]

    @pl.when(j == 0)
    def _():
        acc_ref[...] = jnp.zeros_like(acc_ref)

    for h in range(PEER_HEADS):
        for ii in range(I1_TILE):
            rows = slice(GATE_ROWS * ii, GATE_ROWS * (ii + 1))
            cb_ref[h, rows, :] = jnp.broadcast_to(cnt1_ref[h, ii:ii + 1, :], (GATE_ROWS, tm)).astype(BF16)
            gb_ref[h, rows, :] = jnp.broadcast_to(g1_ref[h, ii:ii + 1, :], (GATE_ROWS, tm)).astype(BF16)

    n_rg = N_KEYS // GATE_ROWS
    na, nb = GATE_BLOCK
    for lc in range(tm // GATE_LANES):
        lanes = slice(GATE_LANES * lc, GATE_LANES * (lc + 1))
        for ii0 in range(0, I1_TILE, na):
            for rg0 in range(0, n_rg, nb):
                accs = [[jnp.zeros((GATE_ROWS, GATE_LANES), BF16) for _ in range(nb)] for _ in range(na)]
                for h in range(PEER_HEADS):
                    rk = [rank2_ref[h, GATE_ROWS * (rg0 + b):GATE_ROWS * (rg0 + b + 1), lanes] for b in range(nb)]
                    g2 = [g2_ref[h, GATE_ROWS * (rg0 + b):GATE_ROWS * (rg0 + b + 1), lanes] for b in range(nb)]
                    for a in range(na):
                        c = cb_ref[h, GATE_ROWS * (ii0 + a):GATE_ROWS * (ii0 + a + 1), lanes]
                        g = gb_ref[h, GATE_ROWS * (ii0 + a):GATE_ROWS * (ii0 + a + 1), lanes]
                        for b in range(nb):
                            accs[a][b] = accs[a][b] + jnp.where(rk[b] < c, g2[b] * g, jnp.zeros_like(g))
                for a in range(na):
                    for b in range(nb):
                        r0 = N_KEYS * (ii0 + a) + GATE_ROWS * (rg0 + b)
                        w_ref[r0:r0 + GATE_ROWS, lanes] = accs[a][b]

    hq = hq_ref[...]
    for rc in range(w_ref.shape[0] // ACT_ROWS):
        rows = slice(ACT_ROWS * rc, ACT_ROWS * (rc + 1))
        act = _dot_tb(eu_ref[rows, :], hq)
        gel = act * (lax.erf(act * (1.0 / math.sqrt(2.0))) + 1.0) * 0.5
        w_ref[rows, :] = w_ref[rows, :] * gel.astype(BF16)
    acc_ref[...] += _dot(evt_ref[...], w_ref[...])

    @pl.when(j == pl.num_programs(1) - 1)
    def _():
        o_ref[...] = acc_ref[...].T


def _experts(hq, eu, evt, layer, cnt1, g1, rank2, g2, tm):
    n = hq.shape[0]
    n_exp = eu.shape[1]
    te = I1_TILE * N_KEYS
    assert tm % GATE_LANES == 0 and te % ACT_ROWS == 0 and n % tm == 0
    return pl.pallas_call(
        _experts_kernel,
        out_shape=jax.ShapeDtypeStruct((n, D), F32),
        grid=(n // tm, n_exp // te),
        in_specs=[pl.BlockSpec((tm, D), lambda i, j: (i, 0)),
                  pl.BlockSpec((None, te, D), lambda i, j: (layer, j, 0)),
                  pl.BlockSpec((None, D, te), lambda i, j: (layer, 0, j)),
                  pl.BlockSpec((PEER_HEADS, I1_TILE, tm), lambda i, j: (0, j, i)),
                  pl.BlockSpec((PEER_HEADS, I1_TILE, tm), lambda i, j: (0, j, i)),
                  pl.BlockSpec((PEER_HEADS, N_KEYS, tm), lambda i, j: (0, 0, i)),
                  pl.BlockSpec((PEER_HEADS, N_KEYS, tm), lambda i, j: (0, 0, i))],
        out_specs=pl.BlockSpec((tm, D), lambda i, j: (i, 0)),
        scratch_shapes=[pltpu.VMEM((D, tm), F32), pltpu.VMEM((te, tm), BF16),
                        pltpu.VMEM((PEER_HEADS, I1_TILE * GATE_ROWS, tm), BF16),
                        pltpu.VMEM((PEER_HEADS, I1_TILE * GATE_ROWS, tm), BF16)],
        compiler_params=_cparams(("arbitrary", "arbitrary")),
        name="peer_experts",
    )(hq, eu, evt, cnt1, g1, rank2, g2)


def _pw1_kernel(x_ref, pe_ref, modp_ref, g_ref, mod_ref, wa_ref, wb_ref, ba_ref, bb_ref, o_ref, x1_ref):
    x1 = x_ref[...] + _mod_part(modp_ref[0], 5) * pe_ref[...]
    x1_ref[...] = x1
    h = _norm_mod(x1, g_ref[...], mod_ref[0], 0, 1).astype(BF16)
    a = _dot(h, wa_ref[...]) + ba_ref[...]
    b = _dot(h, wb_ref[...]) + bb_ref[...]
    o_ref[...] = a * jax.nn.sigmoid(b)


def _pw1(x, pe, mod_prev, g, mod, wa, wb, ba, bb, rowmap, tm):
    n = x.shape[0]
    c2 = lambda i: (0, 0)
    tok = pl.BlockSpec((tm, D), lambda i: (i, 0))
    return pl.pallas_call(
        _pw1_kernel,
        out_shape=(jax.ShapeDtypeStruct((n, D), F32), jax.ShapeDtypeStruct((n, D), F32)),
        grid=(n // tm,),
        in_specs=[tok, tok,
                  pl.BlockSpec((1, 1, 6 * D), rowmap),
                  pl.BlockSpec((1, D), c2),
                  pl.BlockSpec((1, 1, 6 * D), rowmap),
                  pl.BlockSpec((D, D), c2), pl.BlockSpec((D, D), c2),
                  pl.BlockSpec((1, D), c2), pl.BlockSpec((1, D), c2)],
        out_specs=(tok, tok),
        compiler_params=_cparams(("arbitrary",)),
        name="conv_pw1_glu",
    )(x, pe, mod_prev, g, mod, wa, wb, ba, bb)


def _conv_kernel(n_ctx_blocks, seg_ctx, seg_lat,
                 z_ref, x_ref, mod_ref, ck_ref, cb_ref, lg_ref, lb_ref, w2_ref, b2_ref, o_ref,
                 pad_ref, sh_ref, acc_ref):
    tm = z_ref.shape[0]
    i = pl.program_id(0)

    def dwconv(seg):
        nseg = tm // seg
        pitch = seg + CONV_PAD
        zeros = jnp.zeros((CONV_PAD, D), F32)
        for s in range(nseg + 1):
            pad_ref[pitch * s:pitch * s + CONV_PAD, :] = zeros
        for s in range(nseg):
            pad_ref[CONV_PAD + pitch * s:CONV_PAD + pitch * s + seg, :] = z_ref[seg * s:seg * (s + 1), :]
        used = pitch * nseg + CONV_PAD
        span = used - SUBLANES
        for ph in range(SUBLANES):
            sh_ref[ph, 0:span, :] = pad_ref[ph:ph + span, :]
        acc = None
        for k in range(CONV_W):
            row0 = CONV_PAD + k - CONV_HALF
            ph = row0 % SUBLANES
            base = row0 - ph
            parts = [sh_ref[ph, base + pitch * s:base + pitch * s + seg, :] for s in range(nseg)]
            xs = parts[0] if nseg == 1 else jnp.concatenate(parts, axis=0)
            term = xs * ck_ref[k:k + 1, :]
            acc = term if acc is None else acc + term
        acc_ref[...] = acc

    @pl.when(i < n_ctx_blocks)
    def _():
        dwconv(seg_ctx)

    @pl.when(i >= n_ctx_blocks)
    def _():
        dwconv(seg_lat)

    z = acc_ref[...] + cb_ref[...]
    m = jnp.mean(z, axis=-1, keepdims=True)
    dv = z - m
    var = jnp.mean(dv * dv, axis=-1, keepdims=True)
    zn = dv * lax.rsqrt(var + LN_EPS) * lg_ref[...] + lb_ref[...]
    act = (zn * jax.nn.sigmoid(zn)).astype(BF16)
    y = _dot(act, w2_ref[...]) + b2_ref[...]
    o_ref[...] = x_ref[...] + _mod_part(mod_ref[0], 2) * y


def _conv(z, x, mod, ck, cb, lg, lb, w2, b2, rowmap, tm, n_ctx_blocks, seg_ctx, seg_lat):
    n = x.shape[0]
    assert tm % seg_ctx == 0 and tm % seg_lat == 0 and CONV_PAD >= CONV_HALF and CONV_PAD % SUBLANES == 0
    pad_rows = tm + (tm // min(seg_ctx, seg_lat) + 1) * CONV_PAD
    c2 = lambda i: (0, 0)
    return pl.pallas_call(
        functools.partial(_conv_kernel, n_ctx_blocks, seg_ctx, seg_lat),
        out_shape=jax.ShapeDtypeStruct((n, D), F32),
        grid=(n // tm,),
        in_specs=[pl.BlockSpec((tm, D), lambda i: (i, 0)),
                  pl.BlockSpec((tm, D), lambda i: (i, 0)),
                  pl.BlockSpec((1, 1, 6 * D), rowmap),
                  pl.BlockSpec(ck.shape, c2),
                  pl.BlockSpec((1, D), c2), pl.BlockSpec((1, D), c2), pl.BlockSpec((1, D), c2),
                  pl.BlockSpec((D, D), c2), pl.BlockSpec((1, D), c2)],
        out_specs=pl.BlockSpec((tm, D), lambda i: (i, 0)),
        scratch_shapes=[pltpu.VMEM((pad_rows, D), F32),
                        pltpu.VMEM((SUBLANES, pad_rows, D), F32),
                        pltpu.VMEM((tm, D), F32)],
        compiler_params=_cparams(("arbitrary",)),
        name="conv_dw_ln_pw2",
    )(z, x, mod, ck, cb, lg, lb, w2, b2)


def _final_kernel(ncb, x_ref, pe_ref, mod_ref, g_ref, oc_ref, ol_ref):
    x = x_ref[...] + _mod_part(mod_ref[0], 5) * pe_ref[...]
    ms = jnp.mean(x * x, axis=-1, keepdims=True)
    y = x * lax.rsqrt(ms + EPS) * g_ref[...]
    i = pl.program_id(0)

    @pl.when(i < ncb)
    def _():
        oc_ref[...] = y

    @pl.when(i >= ncb)
    def _():
        ol_ref[...] = y


def _final_norm(x, pe, mod, g, rowmap, tm, n_ctx_tokens):
    n = x.shape[0]
    tok = pl.BlockSpec((tm, D), lambda i: (i, 0))
    return pl.pallas_call(
        functools.partial(_final_kernel, n_ctx_tokens // tm),
        out_shape=(jax.ShapeDtypeStruct((n_ctx_tokens, D), F32),
                   jax.ShapeDtypeStruct((n - n_ctx_tokens, D), F32)),
        grid=(n // tm,),
        in_specs=[tok, tok, pl.BlockSpec((1, 1, 6 * D), rowmap), pl.BlockSpec((1, D), lambda i: (0, 0))],
        out_specs=_stream_specs(n_ctx_tokens, tm),
        compiler_params=_cparams(("arbitrary",)),
        name="final_norm",
    )(x, pe, mod, g)


def _peer_layer(x, l, mod_l, norm2, w_query, sub_keys, eu_all, evt_all, rowmap_of):
    hq, q = _query(x, norm2[l][None, :], mod_l, w_query[l].astype(BF16), rowmap_of(512), 512)
    cnt1, g1, rank2, g2 = _topk(q, sub_keys[l], 512)
    return _experts(hq, eu_all, evt_all, l, cnt1, g1, rank2, g2, 1024)


def kernel(x_prompt, x_sample, state_rwkv, c, c_ctx, w_mod, b_mod, norm1, norm2, norm_f, w_in_ab, mu_a, w0, w2, a0, a2, g2, k_k, k_a, r_k, lnx_g, lnx_b, lnv_g, lnv_b, w_s, b_s, w_out_ab, w_pw1, b_pw1, conv_k, conv_b, lnc_g, lnc_b, w_pw2, b_pw2, w_query, sub_keys, expert_u, expert_v):
    n_ctx, ctx_len, _ = x_prompt.shape
    n_lat, lat_len, _ = x_sample.shape
    nc_tok = n_ctx * ctx_len
    grid_w = 64
    xc = x_prompt.reshape(nc_tok, D)
    xl = x_sample.reshape(n_lat * lat_len, D)
    eu_all = expert_u.astype(BF16)
    evt_all = jnp.swapaxes(expert_v.astype(BF16), 1, 2)

    cvec8 = jnp.zeros((8, D), F32).at[0].set(c_ctx).at[1:1 + n_lat].set(c)
    mod = _modulation(cvec8, w_mod, b_mod)
    mods = [mod[l].reshape(8, 1, 6 * D) for l in range(mod.shape[0])]
    rowmap_of = lambda tm: _mod_row_map(nc_tok, lat_len, tm)

    w_in = w_in_ab[0]
    za, ub = _inproj(xc, xl, norm1[0][None, :], mods[0], w_in[:, :A_COLS].astype(BF16),
                     w_in[:, A_COLS:].astype(BF16), rowmap_of(512), 512)
    row = lambda a: a.reshape(1, -1)
    pad_lora = lambda w, off: jnp.zeros((2, 128, A_W), F32).at[0, :HD].set(w[0]).at[1, HD:].set(w[1])
    heads = jnp.arange(A_W) // HD
    hsum = (heads[:, None] == heads[None, :]).astype(BF16)
    rparams = (row(mu_a[0]), w0[0], pad_lora(w2[0], 0), a0[0], pad_lora(a2[0], 0), g2[0],
               row(k_k[0]), row(k_a[0]), row(r_k[0]), row(lnx_g[0]), row(lnx_b[0]), hsum)
    zero_state = jnp.zeros((n_ctx, 2 * NPAIR, 128, 128), F32)
    ya_c, sfin = _rwkv(za, zero_state, ctx_len, 0, rparams)
    ya_s, _ = _rwkv(za, _states_to_blockdiag(state_rwkv[:, 0]), lat_len, nc_tok, rparams)
    ya = jnp.concatenate([ya_c, ya_s], axis=0)
    bsb = jnp.broadcast_to(b_s[0][:, :, None], (b_s.shape[1], GD, GD))
    x = _post(xc, xl, ya, ub, mods[0], row(lnv_g[0]), row(lnv_b[0]), w_s[0], bsb,
              w_out_ab[0].astype(BF16), rowmap_of(512), 512)
    pe = _peer_layer(x, 0, mods[0], norm2, w_query, sub_keys, eu_all, evt_all, rowmap_of)

    glu, x = _pw1(x, pe, mods[0], norm1[1][None, :], mods[1], w_pw1[0][:, :D].astype(BF16),
                  w_pw1[0][:, D:].astype(BF16), row(b_pw1[0][:D]), row(b_pw1[0][D:]), rowmap_of(512), 512)
    ck = jnp.zeros((32, D), F32).at[:CONV_W].set(conv_k[0])
    x = _conv(glu, x, mods[1], ck, row(conv_b[0]), row(lnc_g[0]), row(lnc_b[0]),
              w_pw2[0].astype(BF16), row(b_pw2[0]), rowmap_of(256), 256,
              nc_tok // 256, ctx_len, grid_w)
    pe = _peer_layer(x, 1, mods[1], norm2, w_query, sub_keys, eu_all, evt_all, rowmap_of)

    yc, yl = _final_norm(x, pe, mods[1], norm_f[None, :], rowmap_of(512), 512, nc_tok)
    y_prompt = yc.reshape(n_ctx, ctx_len, D)
    y_sample = yl.reshape(n_lat, lat_len, D)
    new_state = _blockdiag_to_states(sfin)[:, None].astype(x_prompt.dtype)
    return (y_prompt, y_sample, new_state)
```

```python
import functools
import math

import jax
import jax.numpy as jnp
from jax import lax
from jax.experimental import pallas as pl
from jax.experimental.pallas import tpu as pltpu

F32 = jnp.float32
BF16 = jnp.bfloat16

D = 1024
HD = 64
CH = 64
A_W = 512
NPAIR = A_W // 128
R_OFF, K_OFF, V_OFF = 0, 512, 1024
WLO_OFF, ALO_OFF, GLO_OFF, A_COLS = 1536, 1664, 1792, 1920
B_W = 512
GD = 128
CONV_W = 31
CONV_HALF = 15
CONV_PAD = 16
SUBLANES = 8
N_KEYS = 128
TOPK = 16
PEER_HEADS = 8
EPS = 1e-6
LN_EPS = 1e-5
GN_EPS = 64e-5
VMEM_LIMIT = 56 * 1024 * 1024


def _cparams(sem):
    return pltpu.CompilerParams(dimension_semantics=sem, vmem_limit_bytes=VMEM_LIMIT)


def _dot(a, b):
    return jnp.dot(a, b, preferred_element_type=F32)


def _dot_tb(a, b):
    return lax.dot_general(a, b, (((1,), (1,)), ((), ())), preferred_element_type=F32)


def _dot_ta(a, b):
    return lax.dot_general(a, b, (((0,), (0,)), ((), ())), preferred_element_type=F32)


def _split2(x):
    hi = x.astype(BF16)
    lo = (x - hi.astype(F32)).astype(BF16)
    return hi, lo


def _split3(x):
    hi = x.astype(BF16)
    r1 = x - hi.astype(F32)
    mid = r1.astype(BF16)
    lo = (r1 - mid.astype(F32)).astype(BF16)
    return hi, mid, lo


def _mm3(a, b, dot=_dot):
    ah, al = _split2(a)
    bh, bl = _split2(b)
    return dot(ah, bh) + dot(ah, bl) + dot(al, bh)


def _bf(x):
    return x.astype(BF16)


def _mm_exact_lhs(m_bf16, x):
    hi, mid, lo = _split3(x)
    return _dot(m_bf16, hi) + _dot(m_bf16, mid) + _dot(m_bf16, lo)


def _mm_exact_rhs(x, m_bf16):
    hi, lo = _split2(x)
    return _dot(hi, m_bf16) + _dot(lo, m_bf16)


def _mod_part(mod_row, j):
    return mod_row[:, j * D:(j + 1) * D]


def _norm_mod(x, g, mod_row, j_shift, j_scale):
    ms = jnp.mean(x * x, axis=-1, keepdims=True)
    y = x * lax.rsqrt(ms + EPS) * g
    return y * (1.0 + _mod_part(mod_row, j_scale)) + _mod_part(mod_row, j_shift)


def _mod_row_map(n_ctx_tokens, seq_tokens, tm):
    ncb = n_ctx_tokens // tm
    per = seq_tokens // tm

    def imap(i, *_):
        return (jnp.where(i < ncb, 0, 1 + (i - ncb) // per), 0, 0)
    return imap


def _mod_kernel(c_ref, w_ref, b_ref, o_ref):
    c = c_ref[...]
    s = c * jax.nn.sigmoid(c)
    o_ref[0] = _mm3(s, w_ref[0]) + b_ref[0]


def _modulation(cvec8, w_mod, b_mod):
    depth, _, n6 = w_mod.shape
    tn = 1536
    return pl.pallas_call(
        _mod_kernel,
        out_shape=jax.ShapeDtypeStruct((depth, 8, n6), F32),
        grid=(depth, n6 // tn),
        in_specs=[pl.BlockSpec((8, D), lambda l, j: (0, 0)),
                  pl.BlockSpec((1, D, tn), lambda l, j: (l, 0, j)),
                  pl.BlockSpec((1, 1, tn), lambda l, j: (l, 0, j))],
        out_specs=pl.BlockSpec((1, 8, tn), lambda l, j: (l, 0, j)),
        compiler_params=_cparams(("arbitrary", "arbitrary")),
        name="modulation",
    )(cvec8, w_mod, b_mod.reshape(depth, 1, n6))


def _stream_specs(n_ctx_tokens, tm):
    ncb = n_ctx_tokens // tm
    return (pl.BlockSpec((tm, D), lambda i: (jnp.minimum(i, ncb - 1), 0)),
            pl.BlockSpec((tm, D), lambda i: (jnp.maximum(i - ncb, 0), 0)))


def _stream_block(ncb, xc_ref, xl_ref):
    return jnp.where(pl.program_id(0) < ncb, xc_ref[...], xl_ref[...])


def _inproj_kernel(ncb, xc_ref, xl_ref, g_ref, mod_ref, wa_ref, wb_ref, za_ref, ub_ref):
    x = _stream_block(ncb, xc_ref, xl_ref)
    h = _norm_mod(x, g_ref[...], mod_ref[0], 0, 1).astype(BF16)
    za_ref[...] = _dot(h, wa_ref[...])
    ub_ref[...] = _dot(h, wb_ref[...])


def _inproj(xc, xl, g, mod, wa, wb, rowmap, tm):
    n = xc.shape[0] + xl.shape[0]
    return pl.pallas_call(
        functools.partial(_inproj_kernel, xc.shape[0] // tm),
        out_shape=(jax.ShapeDtypeStruct((n, A_COLS), F32), jax.ShapeDtypeStruct((n, 2 * B_W), F32)),
        grid=(n // tm,),
        in_specs=[*_stream_specs(xc.shape[0], tm),
                  pl.BlockSpec((1, D), lambda i: (0, 0)),
                  pl.BlockSpec((1, 1, 6 * D), rowmap),
                  pl.BlockSpec((D, A_COLS), lambda i: (0, 0)),
                  pl.BlockSpec((D, 2 * B_W), lambda i: (0, 0))],
        out_specs=(pl.BlockSpec((tm, A_COLS), lambda i: (i, 0)),
                   pl.BlockSpec((tm, 2 * B_W), lambda i: (i, 0))),
        compiler_params=_cparams(("arbitrary",)),
        name="inproj",
    )(xc, xl, g, mod, wa, wb)


def _softplus(x):
    return jnp.maximum(x, 0.0) + jnp.log1p(jnp.exp(-jnp.abs(x)))


def _rwkv_kernel(za_ref, s0_ref, mu_ref, w0_ref, w2_ref, a0_ref, a2_ref, g2_ref, kk_ref, ka_ref,
                 rk_ref, lng_ref, lnb_ref, hsum_ref, ya_ref, sfin_ref,
                 prep_ref, gam_ref, st_ref, y_ref, bon_ref, gate_ref):
    t_len = za_ref.shape[0]
    nc = t_len // CH
    mu = mu_ref[...]
    hsum = hsum_ref[...]
    row_c = lax.broadcasted_iota(jnp.int32, (CH, A_COLS), 0)
    ri = lax.broadcasted_iota(jnp.int32, (CH, CH), 0)
    ci = lax.broadcasted_iota(jnp.int32, (CH, CH), 1)
    tri_f = (ci <= ri).astype(BF16)
    tri_b = (ci >= ri).astype(BF16)
    r2 = lax.broadcasted_iota(jnp.int32, (2 * CH, 2 * CH), 0)
    c2 = lax.broadcasted_iota(jnp.int32, (2 * CH, 2 * CH), 1)
    r2m = jnp.bitwise_and(r2, CH - 1)
    c2m = jnp.bitwise_and(c2, CH - 1)
    eye2 = (r2 == c2).astype(F32)
    lane_lo = lax.broadcasted_iota(jnp.int32, (CH, 2 * HD), 1) < HD

    st_ref[...] = s0_ref[0]

    def stack(x):
        return jnp.concatenate([jnp.where(lane_lo, x, 0.0), jnp.where(lane_lo, 0.0, x)], axis=0)

    def mixed_chunk(c):
        start = pl.multiple_of(c * CH, CH)
        zc = za_ref[pl.ds(start, CH), :]
        p0 = pl.multiple_of(jnp.maximum(start - 8, 0), 8)
        n0 = pl.multiple_of(jnp.minimum(start + CH, t_len - 8), 8)
        prow = za_ref[pl.ds(p0, 8), :][7:8, :] * (start > 0).astype(F32)
        nrow = za_ref[pl.ds(n0, 8), :][0:1, :] * (start + CH < t_len).astype(F32)
        prev = jnp.where(row_c == 0, prow, pltpu.roll(zc, 1, 0))
        nxt = jnp.where(row_c == CH - 1, nrow, pltpu.roll(zc, CH - 1, 0))
        return zc + mu * (0.5 * (prev + nxt) - zc), start

    def prep(c, d):
        zm, start = mixed_chunk(c)
        r = zm[:, R_OFF:R_OFF + A_W]
        k = zm[:, K_OFF:K_OFF + A_W]
        v = zm[:, V_OFF:V_OFF + A_W]
        wlo2 = zm[:, WLO_OFF:WLO_OFF + 128]
        alo2 = zm[:, ALO_OFF:ALO_OFF + 128]
        kkr = k * kk_ref[...]
        n2 = _mm_exact_rhs(kkr * kkr, hsum)
        kk = kkr / jnp.maximum(jnp.sqrt(n2), 1e-12)
        wl = w0_ref[d:d + 1, :] + _dot(_bf(jnp.tanh(wlo2)), _bf(w2_ref[d]))
        ld = -jnp.exp(-_softplus(-wl) - 0.5)
        ag = jax.nn.sigmoid(a0_ref[d:d + 1, :] + _dot(_bf(alo2), _bf(a2_ref[d])))
        kd = k * (1.0 + (ag - 1.0) * ka_ref[...])
        bv = kk * ag
        cum = _mm_exact_lhs(tri_f if d == 0 else tri_b, ld)
        gam = jnp.exp(cum)
        gam_ex = jnp.exp(cum - ld)
        igam = jnp.exp(-cum)
        ops = (-kk * gam_ex, r * gam, bv * igam, kd * igam, v)
        for j, o in enumerate(ops):
            for p in range(NPAIR):
                prep_ref[d, j, p] = o[:, 128 * p:128 * (p + 1)]
        gtot = gam[CH - 1:CH, :] if d == 0 else gam[0:1, :]
        for p in range(NPAIR):
            gam_ref[d, p] = jnp.broadcast_to(gtot[:, 128 * p:128 * (p + 1)], (8, 128))
        bon_ref[d, pl.ds(start, CH), :] = _mm_exact_rhs(r * kd * rk_ref[...], hsum) * v
        if d == 0:
            glo = zm[:, GLO_OFF:GLO_OFF + 128]
            gate_ref[pl.ds(start, CH), :] = _dot(_bf(jax.nn.sigmoid(glo)), _bf(g2_ref[...]))

    def solve_all(i):
        n2 = 2 * CH
        ps = range(2 * NPAIR)
        dq = [q // NPAIR for q in ps]
        pq = [q % NPAIR for q in ps]
        start = [pl.multiple_of(i * CH, CH), pl.multiple_of((nc - 1 - i) * CH, CH)]
        order = [r2m - c2m, c2m - r2m]
        a_s = [stack(prep_ref[dq[p], 0, pq[p]]) for p in ps]
        r_s = [stack(prep_ref[dq[p], 1, pq[p]]) for p in ps]
        b_h = [_bf(stack(prep_ref[dq[p], 2, pq[p]])) for p in ps]
        k_h = [_bf(stack(prep_ref[dq[p], 3, pq[p]])) for p in ps]
        v_h = [_bf(stack(prep_ref[dq[p], 4, pq[p]])) for p in ps]
        sc = [_dot_tb(_bf(jnp.concatenate([a_s[p], r_s[p]], axis=0)),
                      jnp.concatenate([b_h[p], k_h[p]], axis=0)) for p in ps]
        sab = [jnp.where(order[dq[p]] > 0, sc[p][:n2, :n2], 0.0) for p in ps]
        sak = [jnp.where(order[dq[p]] > 0, sc[p][:n2, n2:], 0.0) for p in ps]
        nrb = [jnp.where(order[dq[p]] >= 0, sc[p][n2:, :n2], 0.0) for p in ps]
        nrk = [jnp.where(order[dq[p]] >= 0, sc[p][n2:, n2:], 0.0) for p in ps]
        pw = []
        for p in ps:
            pw_h = _bf(sab[p])
            pw.append(_dot(pw_h, pw_h))
        tinv = [eye2 + sab[p] for p in ps]
        n_dbl = int(math.log2(CH))
        for kk in range(1, n_dbl):
            for p in ps:
                pw_h = _bf(pw[p])
                if kk < n_dbl - 1:
                    both = _dot(pw_h, jnp.concatenate([pw_h, _bf(tinv[p])], axis=1))
                    pw[p] = both[:, :n2]
                    tinv[p] = tinv[p] + both[:, n2:]
                else:
                    tinv[p] = tinv[p] + _dot(pw_h, _bf(tinv[p]))
        sakv = [_dot(_bf(sak[p]), v_h[p]) for p in ps]
        au_h = [_bf(_dot(_bf(tinv[p]), _bf(jnp.concatenate([a_s[p], sakv[p]], axis=1))))
                for p in ps]
        ry = [_dot(_bf(nrb[p]), au_h[p]) for p in ps]
        nv = [_dot(_bf(nrk[p]), v_h[p]) for p in ps]
        gh = [_dot_ta(au_h[p], b_h[p]) for p in ps]
        vk = [_dot_ta(v_h[p], k_h[p]) for p in ps]
        s_h = [_bf(st_ref[p]) for p in ps]
        ys = [_dot_tb(_bf(r_s[p] + ry[p][:, :n2]), s_h[p]) for p in ps]
        for p in ps:
            gt = gam_ref[dq[p], pq[p]][0:1, :]
            g_t = (eye2 + gh[p][:n2, :]) * gt
            h_t = (gh[p][n2:, :] + vk[p]) * gt
            y_all = ys[p] + ry[p][:, n2:] + nv[p]
            y_ref[dq[p], pq[p], pl.ds(start[dq[p]], CH), :] = y_all[:CH, :] + y_all[CH:, :]
            st_ref[p] = _dot(s_h[p], _bf(g_t)) + h_t

    def chunk_body(i, carry):
        prep(i, 0)
        prep(nc - 1 - i, 1)
        solve_all(i)
        return carry

    lax.fori_loop(0, nc, chunk_body, 0)

    def out_body(c, carry):
        start = pl.multiple_of(c * CH, CH)
        rows = pl.ds(start, CH)
        ysum = jnp.concatenate([y_ref[0, p, rows, :] + y_ref[1, p, rows, :] for p in range(NPAIR)], axis=1)
        mean = _mm_exact_rhs(ysum, hsum) * (1.0 / HD)
        dev = ysum - mean
        var = _mm_exact_rhs(dev * dev, hsum) * (1.0 / HD)
        gn = dev * lax.rsqrt(var + GN_EPS) * lng_ref[...] + lnb_ref[...]
        bonus = bon_ref[0, rows, :] + bon_ref[1, rows, :]
        ya_ref[rows, :] = ((gn + bonus) * gate_ref[rows, :]).astype(ya_ref.dtype)
        return carry

    lax.fori_loop(0, nc, out_body, 0)
    sfin_ref[0] = st_ref[...]


def _rwkv(za, s0bd, t_len, first_token, params):
    n_seq = s0bd.shape[0]
    blk0 = first_token // t_len
    nq = 2 * NPAIR
    full = lambda a: pl.BlockSpec(a.shape, lambda s, _n=a.ndim: (0,) * _n)
    return pl.pallas_call(
        _rwkv_kernel,
        out_shape=(jax.ShapeDtypeStruct((n_seq * t_len, A_W), BF16),
                   jax.ShapeDtypeStruct((n_seq, nq, 128, 128), F32)),
        grid=(n_seq,),
        in_specs=[pl.BlockSpec((t_len, A_COLS), lambda s: (s + blk0, 0)),
                  pl.BlockSpec((1, nq, 128, 128), lambda s: (s, 0, 0, 0))] + [full(a) for a in params],
        out_specs=(pl.BlockSpec((t_len, A_W), lambda s: (s, 0)),
                   pl.BlockSpec((1, nq, 128, 128), lambda s: (s, 0, 0, 0))),
        scratch_shapes=[pltpu.VMEM((2, 5, NPAIR, CH, 128), F32),
                        pltpu.VMEM((2, NPAIR, 8, 128), F32),
                        pltpu.VMEM((nq, 128, 128), F32),
                        pltpu.VMEM((2, NPAIR, t_len, 128), F32),
                        pltpu.VMEM((2, t_len, A_W), F32),
                        pltpu.VMEM((t_len, A_W), F32)],
        compiler_params=_cparams(("arbitrary",)),
        name="rwkv_t%d" % t_len,
    )(za, s0bd, *params)


def _states_to_blockdiag(s):
    n = s.shape[0]
    s = s.reshape(n, 2, NPAIR, 2, HD, HD)
    z = jnp.zeros_like(s[:, :, :, 0])
    top = jnp.concatenate([s[:, :, :, 0], z], axis=-1)
    bot = jnp.concatenate([z, s[:, :, :, 1]], axis=-1)
    return jnp.concatenate([top, bot], axis=-2).reshape(n, 2 * NPAIR, 128, 128)


def _blockdiag_to_states(b):
    n = b.shape[0]
    b = b.reshape(n, 2, NPAIR, 128, 128)
    s = jnp.stack([b[..., :HD, :HD], b[..., HD:, HD:]], axis=3)
    return s.reshape(n, 2, 2 * NPAIR, HD, HD)


def _post_kernel(ncb, xc_ref, xl_ref, ya_ref, ub_ref, mod_ref, lvg_ref, lvb_ref, ws_ref, bs_ref, wo_ref, o_ref):
    tm = xc_ref.shape[0]
    ub = ub_ref[...]
    parts = []
    for g in range(B_W // GD):
        u = ub[:, GD * g:GD * (g + 1)]
        vb = ub[:, B_W + GD * g:B_W + GD * (g + 1)]
        m = jnp.mean(vb, axis=-1, keepdims=True)
        dv = vb - m
        var = jnp.mean(dv * dv, axis=-1, keepdims=True)
        vn = dv * lax.rsqrt(var + LN_EPS) * lvg_ref[:, GD * g:GD * (g + 1)] + lvb_ref[:, GD * g:GD * (g + 1)]
        sp = jnp.concatenate(
            [_mm3(ws_ref[g], vn[GD * c:GD * (c + 1), :]) + bs_ref[g] for c in range(tm // GD)], axis=0)
        parts.append((u * sp).astype(BF16))
    y = jnp.concatenate([ya_ref[...]] + parts, axis=1)
    o_ref[...] = _stream_block(ncb, xc_ref, xl_ref) + _mod_part(mod_ref[0], 2) * _dot(y, wo_ref[...])


def _post(xc, xl, ya, ub, mod, lvg, lvb, ws, bsb, wo, rowmap, tm):
    n = xc.shape[0] + xl.shape[0]
    c2 = lambda i: (0, 0)
    c3 = lambda i: (0, 0, 0)
    return pl.pallas_call(
        functools.partial(_post_kernel, xc.shape[0] // tm),
        out_shape=jax.ShapeDtypeStruct((n, D), F32),
        grid=(n // tm,),
        in_specs=[*_stream_specs(xc.shape[0], tm),
                  pl.BlockSpec((tm, A_W), lambda i: (i, 0)),
                  pl.BlockSpec((tm, 2 * B_W), lambda i: (i, 0)),
                  pl.BlockSpec((1, 1, 6 * D), rowmap),
                  pl.BlockSpec((1, B_W), c2), pl.BlockSpec((1, B_W), c2),
                  pl.BlockSpec(ws.shape, c3), pl.BlockSpec(bsb.shape, c3),
                  pl.BlockSpec((D, D), c2)],
        out_specs=pl.BlockSpec((tm, D), lambda i: (i, 0)),
        compiler_params=_cparams(("arbitrary",)),
        name="gmlp_outproj",
    )(xc, xl, ya, ub, mod, lvg, lvb, ws, bsb, wo)


def _query_kernel(x_ref, g_ref, mod_ref, wq_ref, hq_ref, q_ref):
    h = _norm_mod(x_ref[...], g_ref[...], mod_ref[0], 3, 4).astype(BF16)
    hq_ref[...] = h
    q_ref[...] = _dot(h, wq_ref[...])


def _query(x, g, mod, wq, rowmap, tm):
    n = x.shape[0]
    nq = wq.shape[1]
    return pl.pallas_call(
        _query_kernel,
        out_shape=(jax.ShapeDtypeStruct((n, D), BF16), jax.ShapeDtypeStruct((n, nq), F32)),
        grid=(n // tm,),
        in_specs=[pl.BlockSpec((tm, D), lambda i: (i, 0)),
                  pl.BlockSpec((1, D), lambda i: (0, 0)),
                  pl.BlockSpec((1, 1, 6 * D), rowmap),
                  pl.BlockSpec((D, nq), lambda i: (0, 0))],
        out_specs=(pl.BlockSpec((tm, D), lambda i: (i, 0)), pl.BlockSpec((tm, nq), lambda i: (i, 0))),
        compiler_params=_cparams(("arbitrary",)),
        name="peer_query",
    )(x, g, mod, wq)


def _extract_top(s, rowid, n_rows, count):
    rank = jnp.full(s.shape, float(count), F32)
    vals = []
    for r in range(count):
        m = jnp.max(s, axis=0, keepdims=True)
        idx = jnp.min(jnp.where(s == m, rowid, float(n_rows)), axis=0, keepdims=True)
        hit = rowid == idx
        rank = jnp.where(hit, float(r), rank)
        s = jnp.where(hit, -jnp.inf, s)
        vals.append(m)
    return rank, vals


_CAND_Q = [TOPK // (p + 1) for p in range(TOPK)]


def _pull_distinct(s, count, want_rank=False):
    vals = []
    rank = jnp.full(s.shape, float(count), F32) if want_rank else None
    for r in range(count):
        m = jnp.max(s, axis=0, keepdims=True)
        hit = s == m
        if want_rank:
            rank = jnp.where(hit, float(r), rank)
        s = jnp.where(hit, -jnp.inf, s)
        vals.append(m)
    return (vals, rank) if want_rank else vals


def _candidates(a, b, tb):
    bmat = jnp.concatenate(b, axis=0)
    amat = jnp.concatenate(a, axis=0)
    row8 = lax.broadcasted_iota(jnp.int32, (8, tb), 0)
    blocks = [a[0] + bmat]
    for p in range(1, 8):
        blocks.append(jnp.where(row8 < _CAND_Q[p], a[p] + bmat[:8, :], -jnp.inf))
    blocks.append(amat[8:, :] + b[0])
    return jnp.concatenate(blocks, axis=0)


def _cand_counts(self32):
    cnt = [jnp.sum(self32[0:16], axis=0, keepdims=True)]
    for p in range(1, 8):
        cnt.append(jnp.sum(self32[8 + 8 * p:16 + 8 * p], axis=0, keepdims=True))
    for p in range(8, 16):
        cnt.append(self32[72 + (p - 8):73 + (p - 8)])
    return cnt


def _topk_kernel(q_ref, sk_ref, cnt1_ref, g1_ref, rank2_ref, g2_ref):
    tb = q_ref.shape[0]
    q = q_ref[...]
    s1 = _mm3(sk_ref[0], q[:, :N_KEYS], _dot_tb)
    s2 = _mm3(sk_ref[1], q[:, N_KEYS:], _dot_tb)

    a = _pull_distinct(s1, TOPK)
    b, rank2 = _pull_distinct(s2, TOPK, want_rank=True)
    cand = _candidates(a, b, tb)
    cthr = _pull_distinct(cand, TOPK)[TOPK - 1]
    sel = cand >= cthr
    top = a[0] + b[0]
    z = jnp.sum(jnp.where(sel, jnp.exp(cand - top), 0.0), axis=0, keepdims=True)
    self32 = sel.astype(F32)
    cnt = _cand_counts(self32)
    cnt1 = jnp.zeros((N_KEYS, tb), F32)
    for p in range(TOPK):
        cnt1 = jnp.where(s1 == a[p], cnt[p], cnt1)
    cnt1_ref[0] = cnt1
    g1_ref[0] = jnp.exp(s1 - a[0]) / z
    rank2_ref[0] = rank2.astype(rank2_ref.dtype)
    g2_ref[0] = jnp.exp(s2 - b[0]).astype(g2_ref.dtype)

    n1 = jnp.sum((s1 >= a[TOPK - 1]).astype(F32), axis=0, keepdims=True)
    n2 = jnp.sum((s2 >= b[TOPK - 1]).astype(F32), axis=0, keepdims=True)
    nc = jnp.sum(self32, axis=0, keepdims=True)
    bad = jnp.abs(n1 - TOPK) + jnp.abs(n2 - TOPK) + jnp.abs(nc - TOPK)

    @pl.when(jnp.max(bad) > 0.0)
    def _():
        rowid = lax.broadcasted_iota(jnp.int32, (N_KEYS, tb), 0).astype(F32)
        rank1, ax = _extract_top(s1, rowid, N_KEYS, TOPK)
        rank2x, bx = _extract_top(s2, rowid, N_KEYS, TOPK)
        candx = _candidates(ax, bx, tb)
        n_cand = candx.shape[0]
        crow = lax.broadcasted_iota(jnp.int32, (n_cand, tb), 0).astype(F32)
        crank, _ = _extract_top(candx, crow, n_cand, TOPK)
        selx = crank < float(TOPK)
        zx = jnp.sum(jnp.where(selx, jnp.exp(candx - top), 0.0), axis=0, keepdims=True)
        cntx = _cand_counts(selx.astype(F32))
        cnt1x = jnp.zeros((N_KEYS, tb), F32)
        for p in range(TOPK):
            cnt1x = jnp.where(rank1 == float(p), cntx[p], cnt1x)
        cnt1_ref[0] = cnt1x
        g1_ref[0] = jnp.exp(s1 - a[0]) / zx
        rank2_ref[0] = rank2x.astype(rank2_ref.dtype)


def _topk(q, sk, tb):
    n = q.shape[0]
    shp = jax.ShapeDtypeStruct((PEER_HEADS, N_KEYS, n), F32)
    shp16 = jax.ShapeDtypeStruct((PEER_HEADS, N_KEYS, n), BF16)
    ospec = pl.BlockSpec((1, N_KEYS, tb), lambda i, h: (h, 0, i))
    return pl.pallas_call(
        _topk_kernel,
        out_shape=(shp, shp, shp16, shp16),
        grid=(n // tb, PEER_HEADS),
        in_specs=[pl.BlockSpec((tb, 2 * N_KEYS), lambda i, h: (i, h)),
                  pl.BlockSpec((2, N_KEYS, N_KEYS), lambda i, h: (0, 0, 0))],
        out_specs=(ospec, ospec, ospec, ospec),
        compiler_params=_cparams(("arbitrary", "arbitrary")),
        name="peer_topk",
    )(q, sk)


I1_TILE = 16


GATE_LANES = 256
GATE_BLOCK = (2, 4)
GATE_ROWS = 16
ACT_ROWS = 256


def _experts_kernel(hq_ref, eu_ref, evt_ref, cnt1_ref, g1_ref, rank2_ref, g2_ref,
                    o_ref, acc_ref, w_ref, cb_ref, gb_ref):
    j = pl.program_id(1)
    tm = hq_ref.shape[0]

    @pl.when(j == 0)
    def _():
        acc_ref[...] = jnp.zeros_like(acc_ref)

    for h in range(PEER_HEADS):
        for ii in range(I1_TILE):
            rows = slice(GATE_ROWS * ii, GATE_ROWS * (ii + 1))
            cb_ref[h, rows, :] = jnp.broadcast_to(cnt1_ref[h, ii:ii + 1, :], (GATE_ROWS, tm)).astype(BF16)
            gb_ref[h, rows, :] = jnp.broadcast_to(g1_ref[h, ii:ii + 1, :], (GATE_ROWS, tm)).astype(BF16)

    n_rg = N_KEYS // GATE_ROWS
    na, nb = GATE_BLOCK
    for lc in range(tm // GATE_LANES):
        lanes = slice(GATE_LANES * lc, GATE_LANES * (lc + 1))
        for ii0 in range(0, I1_TILE, na):
            for rg0 in range(0, n_rg, nb):
                accs = [[jnp.zeros((GATE_ROWS, GATE_LANES), BF16) for _ in range(nb)] for _ in range(na)]
                for h in range(PEER_HEADS):
                    rk = [rank2_ref[h, GATE_ROWS * (rg0 + b):GATE_ROWS * (rg0 + b + 1), lanes] for b in range(nb)]
                    g2 = [g2_ref[h, GATE_ROWS * (rg0 + b):GATE_ROWS * (rg0 + b + 1), lanes] for b in range(nb)]
                    for a in range(na):
                        c = cb_ref[h, GATE_ROWS * (ii0 + a):GATE_ROWS * (ii0 + a + 1), lanes]
                        g = gb_ref[h, GATE_ROWS * (ii0 + a):GATE_ROWS * (ii0 + a + 1), lanes]
                        for b in range(nb):
                            accs[a][b] = accs[a][b] + jnp.where(rk[b] < c, g2[b] * g, jnp.zeros_like(g))
                for a in range(na):
                    for b in range(nb):
                        r0 = N_KEYS * (ii0 + a) + GATE_ROWS * (rg0 + b)
                        w_ref[r0:r0 + GATE_ROWS, lanes] = accs[a][b]

    hq = hq_ref[...]
    for rc in range(w_ref.shape[0] // ACT_ROWS):
        rows = slice(ACT_ROWS * rc, ACT_ROWS * (rc + 1))
        act = _dot_tb(eu_ref[rows, :], hq)
        gel = act * (lax.erf(act * (1.0 / math.sqrt(2.0))) + 1.0) * 0.5
        w_ref[rows, :] = w_ref[rows, :] * gel.astype(BF16)
    acc_ref[...] += _dot(evt_ref[...], w_ref[...])

    @pl.when(j == pl.num_programs(1) - 1)
    def _():
        o_ref[...] = acc_ref[...].T


def _experts(hq, eu, evt, layer, cnt1, g1, rank2, g2, tm):
    n = hq.shape[0]
    n_exp = eu.shape[1]
    te = I1_TILE * N_KEYS
    assert tm % GATE_LANES == 0 and te % ACT_ROWS == 0 and n % tm == 0
    return pl.pallas_call(
        _experts_kernel,
        out_shape=jax.ShapeDtypeStruct((n, D), F32),
        grid=(n // tm, n_exp // te),
        in_specs=[pl.BlockSpec((tm, D), lambda i, j: (i, 0)),
                  pl.BlockSpec((None, te, D), lambda i, j: (layer, j, 0)),
                  pl.BlockSpec((None, D, te), lambda i, j: (layer, 0, j)),
                  pl.BlockSpec((PEER_HEADS, I1_TILE, tm), lambda i, j: (0, j, i)),
                  pl.BlockSpec((PEER_HEADS, I1_TILE, tm), lambda i, j: (0, j, i)),
                  pl.BlockSpec((PEER_HEADS, N_KEYS, tm), lambda i, j: (0, 0, i)),
                  pl.BlockSpec((PEER_HEADS, N_KEYS, tm), lambda i, j: (0, 0, i))],
        out_specs=pl.BlockSpec((tm, D), lambda i, j: (i, 0)),
        scratch_shapes=[pltpu.VMEM((D, tm), F32), pltpu.VMEM((te, tm), BF16),
                        pltpu.VMEM((PEER_HEADS, I1_TILE * GATE_ROWS, tm), BF16),
                        pltpu.VMEM((PEER_HEADS, I1_TILE * GATE_ROWS, tm), BF16)],
        compiler_params=_cparams(("arbitrary", "arbitrary")),
        name="peer_experts",
    )(hq, eu, evt, cnt1, g1, rank2, g2)


def _pw1_kernel(x_ref, pe_ref, modp_ref, g_ref, mod_ref, wa_ref, wb_ref, ba_ref, bb_ref, o_ref, x1_ref):
    x1 = x_ref[...] + _mod_part(modp_ref[0], 5) * pe_ref[...]
    x1_ref[...] = x1
    h = _norm_mod(x1, g_ref[...], mod_ref[0], 0, 1).astype(BF16)
    a = _dot(h, wa_ref[...]) + ba_ref[...]
    b = _dot(h, wb_ref[...]) + bb_ref[...]
    o_ref[...] = a * jax.nn.sigmoid(b)


def _pw1(x, pe, mod_prev, g, mod, wa, wb, ba, bb, rowmap, tm):
    n = x.shape[0]
    c2 = lambda i: (0, 0)
    tok = pl.BlockSpec((tm, D), lambda i: (i, 0))
    return pl.pallas_call(
        _pw1_kernel,
        out_shape=(jax.ShapeDtypeStruct((n, D), F32), jax.ShapeDtypeStruct((n, D), F32)),
        grid=(n // tm,),
        in_specs=[tok, tok,
                  pl.BlockSpec((1, 1, 6 * D), rowmap),
                  pl.BlockSpec((1, D), c2),
                  pl.BlockSpec((1, 1, 6 * D), rowmap),
                  pl.BlockSpec((D, D), c2), pl.BlockSpec((D, D), c2),
                  pl.BlockSpec((1, D), c2), pl.BlockSpec((1, D), c2)],
        out_specs=(tok, tok),
        compiler_params=_cparams(("arbitrary",)),
        name="conv_pw1_glu",
    )(x, pe, mod_prev, g, mod, wa, wb, ba, bb)


def _conv_kernel(n_ctx_blocks, seg_ctx, seg_lat,
                 z_ref, x_ref, mod_ref, ck_ref, cb_ref, lg_ref, lb_ref, w2_ref, b2_ref, o_ref,
                 pad_ref, sh_ref, acc_ref):
    tm = z_ref.shape[0]
    i = pl.program_id(0)

    def dwconv(seg):
        nseg = tm // seg
        pitch = seg + CONV_PAD
        zeros = jnp.zeros((CONV_PAD, D), F32)
        for s in range(nseg + 1):
            pad_ref[pitch * s:pitch * s + CONV_PAD, :] = zeros
        for s in range(nseg):
            pad_ref[CONV_PAD + pitch * s:CONV_PAD + pitch * s + seg, :] = z_ref[seg * s:seg * (s + 1), :]
        used = pitch * nseg + CONV_PAD
        span = used - SUBLANES
        for ph in range(SUBLANES):
            sh_ref[ph, 0:span, :] = pad_ref[ph:ph + span, :]
        acc = None
        for k in range(CONV_W):
            row0 = CONV_PAD + k - CONV_HALF
            ph = row0 % SUBLANES
            base = row0 - ph
            parts = [sh_ref[ph, base + pitch * s:base + pitch * s + seg, :] for s in range(nseg)]
            xs = parts[0] if nseg == 1 else jnp.concatenate(parts, axis=0)
            term = xs * ck_ref[k:k + 1, :]
            acc = term if acc is None else acc + term
        acc_ref[...] = acc

    @pl.when(i < n_ctx_blocks)
    def _():
        dwconv(seg_ctx)

    @pl.when(i >= n_ctx_blocks)
    def _():
        dwconv(seg_lat)

    z = acc_ref[...] + cb_ref[...]
    m = jnp.mean(z, axis=-1, keepdims=True)
    dv = z - m
    var = jnp.mean(dv * dv, axis=-1, keepdims=True)
    zn = dv * lax.rsqrt(var + LN_EPS) * lg_ref[...] + lb_ref[...]
    act = (zn * jax.nn.sigmoid(zn)).astype(BF16)
    y = _dot(act, w2_ref[...]) + b2_ref[...]
    o_ref[...] = x_ref[...] + _mod_part(mod_ref[0], 2) * y


def _conv(z, x, mod, ck, cb, lg, lb, w2, b2, rowmap, tm, n_ctx_blocks, seg_ctx, seg_lat):
    n = x.shape[0]
    assert tm % seg_ctx == 0 and tm % seg_lat == 0 and CONV_PAD >= CONV_HALF and CONV_PAD % SUBLANES == 0
    pad_rows = tm + (tm // min(seg_ctx, seg_lat) + 1) * CONV_PAD
    c2 = lambda i: (0, 0)
    return pl.pallas_call(
        functools.partial(_conv_kernel, n_ctx_blocks, seg_ctx, seg_lat),
        out_shape=jax.ShapeDtypeStruct((n, D), F32),
        grid=(n // tm,),
        in_specs=[pl.BlockSpec((tm, D), lambda i: (i, 0)),
                  pl.BlockSpec((tm, D), lambda i: (i, 0)),
                  pl.BlockSpec((1, 1, 6 * D), rowmap),
                  pl.BlockSpec(ck.shape, c2),
                  pl.BlockSpec((1, D), c2), pl.BlockSpec((1, D), c2), pl.BlockSpec((1, D), c2),
                  pl.BlockSpec((D, D), c2), pl.BlockSpec((1, D), c2)],
        out_specs=pl.BlockSpec((tm, D), lambda i: (i, 0)),
        scratch_shapes=[pltpu.VMEM((pad_rows, D), F32),
                        pltpu.VMEM((SUBLANES, pad_rows, D), F32),
                        pltpu.VMEM((tm, D), F32)],
        compiler_params=_cparams(("arbitrary",)),
        name="conv_dw_ln_pw2",
    )(z, x, mod, ck, cb, lg, lb, w2, b2)


def _final_kernel(ncb, x_ref, pe_ref, mod_ref, g_ref, oc_ref, ol_ref):
    x = x_ref[...] + _mod_part(mod_ref[0], 5) * pe_ref[...]
    ms = jnp.mean(x * x, axis=-1, keepdims=True)
    y = x * lax.rsqrt(ms + EPS) * g_ref[...]
    i = pl.program_id(0)

    @pl.when(i < ncb)
    def _():
        oc_ref[...] = y

    @pl.when(i >= ncb)
    def _():
        ol_ref[...] = y


def _final_norm(x, pe, mod, g, rowmap, tm, n_ctx_tokens):
    n = x.shape[0]
    tok = pl.BlockSpec((tm, D), lambda i: (i, 0))
    return pl.pallas_call(
        functools.partial(_final_kernel, n_ctx_tokens // tm),
        out_shape=(jax.ShapeDtypeStruct((n_ctx_tokens, D), F32),
                   jax.ShapeDtypeStruct((n - n_ctx_tokens, D), F32)),
        grid=(n // tm,),
        in_specs=[tok, tok, pl.BlockSpec((1, 1, 6 * D), rowmap), pl.BlockSpec((1, D), lambda i: (0, 0))],
        out_specs=_stream_specs(n_ctx_tokens, tm),
        compiler_params=_cparams(("arbitrary",)),
        name="final_norm",
    )(x, pe, mod, g)


def _peer_layer(x, l, mod_l, norm2, w_query, sub_keys, eu_all, evt_all, rowmap_of):
    hq, q = _query(x, norm2[l][None, :], mod_l, w_query[l].astype(BF16), rowmap_of(512), 512)
    cnt1, g1, rank2, g2 = _topk(q, sub_keys[l], 512)
    return _experts(hq, eu_all, evt_all, l, cnt1, g1, rank2, g2, 512)


def kernel(x_prompt, x_sample, state_rwkv, c, c_ctx, w_mod, b_mod, norm1, norm2, norm_f, w_in_ab, mu_a, w0, w2, a0, a2, g2, k_k, k_a, r_k, lnx_g, lnx_b, lnv_g, lnv_b, w_s, b_s, w_out_ab, w_pw1, b_pw1, conv_k, conv_b, lnc_g, lnc_b, w_pw2, b_pw2, w_query, sub_keys, expert_u, expert_v):
    n_ctx, ctx_len, _ = x_prompt.shape
    n_lat, lat_len, _ = x_sample.shape
    nc_tok = n_ctx * ctx_len
    grid_w = 64
    xc = x_prompt.reshape(nc_tok, D)
    xl = x_sample.reshape(n_lat * lat_len, D)
    eu_all = expert_u.astype(BF16)
    evt_all = jnp.swapaxes(expert_v.astype(BF16), 1, 2)

    cvec8 = jnp.zeros((8, D), F32).at[0].set(c_ctx).at[1:1 + n_lat].set(c)
    mod = _modulation(cvec8, w_mod, b_mod)
    mods = [mod[l].reshape(8, 1, 6 * D) for l in range(mod.shape[0])]
    rowmap_of = lambda tm: _mod_row_map(nc_tok, lat_len, tm)

    w_in = w_in_ab[0]
    za, ub = _inproj(xc, xl, norm1[0][None, :], mods[0], w_in[:, :A_COLS].astype(BF16),
                     w_in[:, A_COLS:].astype(BF16), rowmap_of(512), 512)
    row = lambda a: a.reshape(1, -1)
    pad_lora = lambda w, off: jnp.zeros((2, 128, A_W), F32).at[0, :HD].set(w[0]).at[1, HD:].set(w[1])
    heads = jnp.arange(A_W) // HD
    hsum = (heads[:, None] == heads[None, :]).astype(BF16)
    rparams = (row(mu_a[0]), w0[0], pad_lora(w2[0], 0), a0[0], pad_lora(a2[0], 0), g2[0],
               row(k_k[0]), row(k_a[0]), row(r_k[0]), row(lnx_g[0]), row(lnx_b[0]), hsum)
    zero_state = jnp.zeros((n_ctx, 2 * NPAIR, 128, 128), F32)
    ya_c, sfin = _rwkv(za, zero_state, ctx_len, 0, rparams)
    ya_s, _ = _rwkv(za, _states_to_blockdiag(state_rwkv[:, 0]), lat_len, nc_tok, rparams)
    ya = jnp.concatenate([ya_c, ya_s], axis=0)
    bsb = jnp.broadcast_to(b_s[0][:, :, None], (b_s.shape[1], GD, GD))
    x = _post(xc, xl, ya, ub, mods[0], row(lnv_g[0]), row(lnv_b[0]), w_s[0], bsb,
              w_out_ab[0].astype(BF16), rowmap_of(512), 512)
    pe = _peer_layer(x, 0, mods[0], norm2, w_query, sub_keys, eu_all, evt_all, rowmap_of)

    glu, x = _pw1(x, pe, mods[0], norm1[1][None, :], mods[1], w_pw1[0][:, :D].astype(BF16),
                  w_pw1[0][:, D:].astype(BF16), row(b_pw1[0][:D]), row(b_pw1[0][D:]), rowmap_of(512), 512)
    ck = jnp.zeros((32, D), F32).at[:CONV_W].set(conv_k[0])
    x = _conv(glu, x, mods[1], ck, row(conv_b[0]), row(lnc_g[0]), row(lnc_b[0]),
              w_pw2[0].astype(BF16), row(b_pw2[0]), rowmap_of(256), 256,
              nc_tok // 256, ctx_len, grid_w)
    pe = _peer_layer(x, 1, mods[1], norm2, w_query, sub_keys, eu_all, evt_all, rowmap_of)

    yc, yl = _final_norm(x, pe, mods[1], norm_f[None, :], rowmap_of(512), 512, nc_tok)
    y_prompt = yc.reshape(n_ctx, ctx_len, D)
    y_sample = yl.reshape(n_lat, lat_len, D)
    new_state = _blockdiag_to_states(sfin)[:, None].astype(x_prompt.dtype)
    return (y_prompt, y_sample, new_state)
```
